```python
import math
import jax
import jax.numpy as jnp
from jax import lax
import numpy as np

D_MODEL = 1024
BATCH = 8
SEQ = 2048
DEPTH = 4
DEC_BATCH = 32
DEC_SEQ = 4
PAST_LEN = 8192
PAGE_SIZE = 128

N_MIXERS = 3
N_HEADS = 24
HEAD_DIM = 64
WIDTH = N_HEADS * HEAD_DIM
N_GROUPS = 3
H_GROUP = N_HEADS // N_GROUPS
DILATED_CONFIGS = ((128, 1), (512, 4), (2048, 16))
A_OUT = H_GROUP * HEAD_DIM
DIFF_DIM = HEAD_DIM // 2
NUM_BUCKETS = 32
REL_MAX_DIST = 2048
Q_BLOCK = 128
EPS = 1e-6
N_A = (DEPTH + 2) // 3
N_B = (DEPTH + 1) // 3
N_C = DEPTH // 3

kernel_name = "hybrid_dilated_diff_stickbreak_decode_step"


def rmsnorm(x, g):
    xf = x.astype(jnp.float32)
    y = xf * lax.rsqrt(jnp.mean(xf * xf, axis=-1, keepdims=True) + EPS)
    return (y * g.astype(jnp.float32)).astype(x.dtype)


def rel_bucket(n):
    max_exact = NUM_BUCKETS // 2
    nf = jnp.maximum(n, 1).astype(jnp.float32)
    large = max_exact + (jnp.log(nf / max_exact) / math.log(REL_MAX_DIST / max_exact)
                         * (NUM_BUCKETS - max_exact)).astype(jnp.int32)
    large = jnp.minimum(large, NUM_BUCKETS - 1)
    return jnp.where(n < max_exact, n, large)


def sweep_query_blocks(fn, q):
    B, T = q.shape[:2]
    qb = Q_BLOCK if T % Q_BLOCK == 0 else T
    nb = T // qb
    q_blk = jnp.swapaxes(q.reshape((B, nb, qb) + q.shape[2:]), 0, 1)
    starts = jnp.arange(nb, dtype=jnp.int32) * qb
    out = lax.map(lambda a: fn(a[0], a[1]), (q_blk, starts))
    return jax.tree_util.tree_map(lambda o: jnp.swapaxes(o, 0, 1).reshape((B, T) + o.shape[3:]), out)


def gather_pages(pool, page_table):
    pages = pool[page_table]
    return pages.reshape((page_table.shape[0], page_table.shape[1] * pool.shape[1]) + pool.shape[2:])


def dilated_group(q, k_ext, v_ext, n_past, window, dilation, bias_g):
    n_keys = window // dilation + 1
    offs = jnp.arange(n_keys, dtype=jnp.int32) * dilation
    bias_j = bias_g[rel_bucket(offs)].astype(jnp.float32).T
    scale = HEAD_DIM ** -0.5

    def block(qb, start):
        qi = n_past + start + jnp.arange(qb.shape[1], dtype=jnp.int32)
        idx = qi[:, None] - offs[None, :]
        valid = idx >= 0
        idx = jnp.maximum(idx, 0)
        kg = jnp.take(k_ext, idx, axis=1, mode='clip')
        vg = jnp.take(v_ext, idx, axis=1, mode='clip')
        logits = jnp.einsum('bqhd,bqjhd->bqhj', qb, kg).astype(jnp.float32) * scale + bias_j
        logits = jnp.where(valid[None, :, None, :], logits, -jnp.inf)
        m = jnp.max(logits, axis=-1, keepdims=True)
        e = jnp.exp(logits - m)
        s = jnp.sum(e, axis=-1)
        o = jnp.einsum('bqhj,bqjhd->bqhd', e, vg.astype(jnp.float32)) / s[..., None]
        return o, m[..., 0] + jnp.log(s)

    return sweep_query_blocks(block, q)


def dilated_mixer(h, w_in, w_out, q_gain, k_gain, rel_bias, past_k, past_v):
    B, T, _ = h.shape
    q, k, v, gate = jnp.split(h @ w_in, [WIDTH, 2 * WIDTH, 3 * WIDTH], axis=-1)
    q = rmsnorm(q.reshape(B, T, N_HEADS, HEAD_DIM), q_gain)
    k = rmsnorm(k.reshape(B, T, N_HEADS, HEAD_DIM), k_gain)
    v = v.reshape(B, T, N_HEADS, HEAD_DIM)
    outs, lses = [], []
    for g, (window, dil) in enumerate(DILATED_CONFIGS):
        hs = slice(g * H_GROUP, (g + 1) * H_GROUP)
        kg, vg = k[:, :, hs], v[:, :, hs]
        if past_k is None:
            k_ext, v_ext, n_past = kg, vg, 0
        else:
            k_ext = jnp.concatenate([past_k[g], kg], axis=1)
            v_ext = jnp.concatenate([past_v[g], vg], axis=1)
            n_past = past_k[g].shape[1]
        o, lse = dilated_group(q[:, :, hs], k_ext, v_ext, n_past, window, dil, rel_bias[:, hs])
        outs.append(o)
        lses.append(lse)
    alpha = jax.nn.softmax(jnp.stack(lses, axis=0), axis=0)
    o = jnp.sum(alpha[..., None] * jnp.stack(outs, axis=0), axis=0)
    o = o.astype(h.dtype).reshape(B, T, A_OUT) * jax.nn.silu(gate)
    return o @ w_out, k, v


def diff_attention(q, k_ext, v_ext, n_past, rel_bias, lam, lambda_init, sub_gain):
    B = q.shape[0]
    Lk = k_ext.shape[1]
    kpos = jnp.arange(Lk, dtype=jnp.int32)
    scale = DIFF_DIM ** -0.5
    v32 = v_ext.astype(jnp.float32)

    def block(qb, start):
        nq = qb.shape[1]
        qi = n_past + start + jnp.arange(nq, dtype=jnp.int32)
        dist = qi[:, None] - kpos[None, :]
        bias = jnp.transpose(rel_bias[rel_bucket(jnp.maximum(dist, 0))], (2, 0, 1)).astype(jnp.float32)
        logits = jnp.einsum('bqnd,bknd->bnqk', qb, k_ext).astype(jnp.float32) * scale
        logits = logits.reshape(B, N_HEADS, 2, nq, Lk) + bias[None, :, None]
        logits = jnp.where(dist >= 0, logits, -jnp.inf)
        p = jax.nn.softmax(logits, axis=-1)
        attn = p[:, :, 0] - lam * p[:, :, 1]
        o = jnp.einsum('bhqk,bkhd->bqhd', attn, v32)
        return rmsnorm(o, sub_gain) * (1.0 - lambda_init)

    return sweep_query_blocks(block, q)


def diff_mixer(h, w_in, w_out, q_gain, k_gain, lq1, lk1, lq2, lk2, sub_gain, rel_bias, lambda_init,
               past_k, past_v):
    B, T, _ = h.shape
    q, k, v, gate = jnp.split(h @ w_in, 4, axis=-1)
    q = rmsnorm(q.reshape(B, T, 2 * N_HEADS, DIFF_DIM), q_gain)
    k = rmsnorm(k.reshape(B, T, 2 * N_HEADS, DIFF_DIM), k_gain)
    v = v.reshape(B, T, N_HEADS, HEAD_DIM)
    if past_k is None:
        k_ext, v_ext, n_past = k, v, 0
    else:
        k_ext = jnp.concatenate([past_k, k], axis=1)
        v_ext = jnp.concatenate([past_v, v], axis=1)
        n_past = past_k.shape[1]
    lam = (jnp.exp(jnp.sum((lq1 * lk1).astype(jnp.float32)))
           - jnp.exp(jnp.sum((lq2 * lk2).astype(jnp.float32))) + lambda_init)
    o = diff_attention(q, k_ext, v_ext, n_past, rel_bias, lam, lambda_init, sub_gain)
    o = o.astype(h.dtype).reshape(B, T, WIDTH) * jax.nn.silu(gate)
    return o @ w_out, k, v


def stick_breaking(q, k_ext, v_ext, n_past):
    Lk = k_ext.shape[1]
    kpos = jnp.arange(Lk, dtype=jnp.int32)
    scale = HEAD_DIM ** -0.5
    v32 = v_ext.astype(jnp.float32)

    def block(qb, start):
        qi = n_past + start + jnp.arange(qb.shape[1], dtype=jnp.int32)
        valid = kpos[None, :] < qi[:, None]
        z = jnp.einsum('bqhd,bkhd->bhqk', qb, k_ext).astype(jnp.float32) * scale
        log_stay = jnp.where(valid, jax.nn.log_sigmoid(-z), 0.0)
        after = lax.cumsum(log_stay, axis=3, reverse=True) - log_stay
        a = jnp.where(valid, jnp.exp(jax.nn.log_sigmoid(z) + after), 0.0)
        return jnp.einsum('bhqk,bkhd->bqhd', a, v32)

    return sweep_query_blocks(block, q)


def stick_mixer(h, w_in, w_out, past_k, past_v):
    B, T, _ = h.shape
    q, k, v, gate = jnp.split(h @ w_in, 4, axis=-1)
    q = q.reshape(B, T, N_HEADS, HEAD_DIM)
    k = k.reshape(B, T, N_HEADS, HEAD_DIM)
    v = v.reshape(B, T, N_HEADS, HEAD_DIM)
    if past_k is None:
        k_ext, v_ext, n_past = k, v, 0
    else:
        k_ext = jnp.concatenate([past_k, k], axis=1)
        v_ext = jnp.concatenate([past_v, v], axis=1)
        n_past = past_k.shape[1]
    o = stick_breaking(q, k_ext, v_ext, n_past)
    o = o.astype(h.dtype).reshape(B, T, WIDTH) * jax.nn.silu(gate)
    return o @ w_out, k, v


def setup_inputs(seed: int = 0) -> dict:
    key = jax.random.key(seed)
    ks = iter(jax.random.split(key, 48))

    def nrm(shape, scale=1.0):
        return jax.random.normal(next(ks), shape, jnp.float32) * scale

    n_pages = PAST_LEN // PAGE_SIZE
    n_used = DEC_BATCH * n_pages
    n_pool = (n_used * 5) // 4
    inp = {}
    inp["x_prompt"] = nrm((BATCH, SEQ, D_MODEL))
    inp["x_sample"] = nrm((DEC_BATCH, DEC_SEQ, D_MODEL))
    for g, (window, _) in enumerate(DILATED_CONFIGS):
        L = min(window, PAST_LEN)
        inp["cache_a_k_g%d" % g] = nrm((N_A, DEC_BATCH, L, H_GROUP, HEAD_DIM))
        inp["cache_a_v_g%d" % g] = nrm((N_A, DEC_BATCH, L, H_GROUP, HEAD_DIM))
    inp["cache_b_k"] = nrm((N_B, n_pool, PAGE_SIZE, 2 * N_HEADS, DIFF_DIM))
    inp["cache_b_v"] = nrm((N_B, n_pool, PAGE_SIZE, N_HEADS, HEAD_DIM))
    inp["cache_c_k"] = nrm((N_C, n_pool, PAGE_SIZE, N_HEADS, HEAD_DIM))
    inp["cache_c_v"] = nrm((N_C, n_pool, PAGE_SIZE, N_HEADS, HEAD_DIM))
    perm = jax.random.permutation(next(ks), n_pool)[:n_used]
    inp["page_table"] = perm.reshape(DEC_BATCH, n_pages).astype(jnp.int32)
    inp["norm_gain"] = 1.0 + nrm((DEPTH, D_MODEL), 0.02)
    inp["rel_bias"] = nrm((NUM_BUCKETS, N_HEADS), 0.3)
    inp["w_in_a"] = nrm((N_A, D_MODEL, 3 * WIDTH + A_OUT), D_MODEL ** -0.5)
    inp["w_out_a"] = nrm((N_A, A_OUT, D_MODEL), A_OUT ** -0.5)
    inp["q_gain_a"] = 1.0 + nrm((N_A, HEAD_DIM), 0.02)
    inp["k_gain_a"] = 1.0 + nrm((N_A, HEAD_DIM), 0.02)
    inp["w_in_b"] = nrm((N_B, D_MODEL, 4 * WIDTH), D_MODEL ** -0.5)
    inp["w_out_b"] = nrm((N_B, WIDTH, D_MODEL), WIDTH ** -0.5)
    inp["q_gain_b"] = 1.0 + nrm((N_B, DIFF_DIM), 0.02)
    inp["k_gain_b"] = 1.0 + nrm((N_B, DIFF_DIM), 0.02)
    inp["lambda_q1"] = nrm((N_B, DIFF_DIM), 0.1)
    inp["lambda_k1"] = nrm((N_B, DIFF_DIM), 0.1)
    inp["lambda_q2"] = nrm((N_B, DIFF_DIM), 0.1)
    inp["lambda_k2"] = nrm((N_B, DIFF_DIM), 0.1)
    inp["sub_gain_b"] = 1.0 + nrm((N_B, HEAD_DIM), 0.02)
    inp["w_in_c"] = nrm((N_C, D_MODEL, 4 * WIDTH), D_MODEL ** -0.5)
    inp["w_out_c"] = nrm((N_C, WIDTH, D_MODEL), WIDTH ** -0.5)
    return inp


def reference(x_prompt, x_sample, cache_a_k_g0, cache_a_v_g0, cache_a_k_g1, cache_a_v_g1,
              cache_a_k_g2, cache_a_v_g2, cache_b_k, cache_b_v, cache_c_k, cache_c_v, page_table,
              norm_gain, rel_bias, w_in_a, w_out_a, q_gain_a, k_gain_a,
              w_in_b, w_out_b, q_gain_b, k_gain_b, lambda_q1, lambda_k1, lambda_q2, lambda_k2, sub_gain_b,
              w_in_c, w_out_c):
    cache_a_k = (cache_a_k_g0, cache_a_k_g1, cache_a_k_g2)
    cache_a_v = (cache_a_v_g0, cache_a_v_g1, cache_a_v_g2)
    a_kp = [[] for _ in range(N_GROUPS)]
    a_vp = [[] for _ in range(N_GROUPS)]
    a_ks, a_vs = [], []
    b_kp, b_vp, b_ks, b_vs = [], [], [], []
    c_kp, c_vp, c_ks, c_vs = [], [], [], []
    y_p, y_s = x_prompt, x_sample
    for i in range(DEPTH):
        kind, j = i % N_MIXERS, i // N_MIXERS
        hp = rmsnorm(y_p, norm_gain[i])
        hs = rmsnorm(y_s, norm_gain[i])
        if kind == 0:
            out_p, kp, vp = dilated_mixer(hp, w_in_a[j], w_out_a[j], q_gain_a[j], k_gain_a[j], rel_bias,
                                          None, None)
            out_s, ks_, vs_ = dilated_mixer(hs, w_in_a[j], w_out_a[j], q_gain_a[j], k_gain_a[j], rel_bias,
                                            tuple(c[j] for c in cache_a_k), tuple(c[j] for c in cache_a_v))
            for g, (window, _) in enumerate(DILATED_CONFIGS):
                hsl = slice(g * H_GROUP, (g + 1) * H_GROUP)
                keep = min(window, SEQ)
                a_kp[g].append(kp[:, SEQ - keep:, hsl])
                a_vp[g].append(vp[:, SEQ - keep:, hsl])
            a_ks.append(ks_)
            a_vs.append(vs_)
        elif kind == 1:
            lambda_init = float(0.8 - 0.6 * math.exp(-0.3 * i))
            past_k = gather_pages(cache_b_k[j], page_table)
            past_v = gather_pages(cache_b_v[j], page_table)
            args = (w_in_b[j], w_out_b[j], q_gain_b[j], k_gain_b[j], lambda_q1[j], lambda_k1[j],
                    lambda_q2[j], lambda_k2[j], sub_gain_b[j], rel_bias, lambda_init)
            out_p, kp, vp = diff_mixer(hp, *args, None, None)
            out_s, ks_, vs_ = diff_mixer(hs, *args, past_k, past_v)
            b_kp.append(kp)
            b_vp.append(vp)
            b_ks.append(ks_)
            b_vs.append(vs_)
        else:
            past_k = gather_pages(cache_c_k[j], page_table)
            past_v = gather_pages(cache_c_v[j], page_table)
            out_p, kp, vp = stick_mixer(hp, w_in_c[j], w_out_c[j], None, None)
            out_s, ks_, vs_ = stick_mixer(hs, w_in_c[j], w_out_c[j], past_k, past_v)
            c_kp.append(kp)
            c_vp.append(vp)
            c_ks.append(ks_)
            c_vs.append(vs_)
        y_p = y_p + out_p
        y_s = y_s + out_s
    return (y_p, y_s,
            jnp.stack(a_kp[0]), jnp.stack(a_vp[0]), jnp.stack(a_kp[1]), jnp.stack(a_vp[1]),
            jnp.stack(a_kp[2]), jnp.stack(a_vp[2]), jnp.stack(a_ks), jnp.stack(a_vs),
            jnp.stack(b_kp), jnp.stack(b_vp), jnp.stack(b_ks), jnp.stack(b_vs),
            jnp.stack(c_kp), jnp.stack(c_vp), jnp.stack(c_ks), jnp.stack(c_vs))
```

```python
import functools
import math

import numpy as np
import jax
import jax.numpy as jnp
from jax import lax
from jax.experimental import pallas as pl
from jax.experimental.pallas import tpu as pltpu

F32 = jnp.float32
BF16 = jnp.bfloat16

D_MODEL = 1024
N_HEADS = 24
HEAD_DIM = 64
WIDTH = N_HEADS * HEAD_DIM
H_GROUP = 8
A_OUT = H_GROUP * HEAD_DIM
DIFF_DIM = HEAD_DIM // 2
DILATED_CONFIGS = ((128, 1), (512, 4), (2048, 16))
NUM_BUCKETS = 32
REL_MAX_DIST = 2048
EPS = 1e-6
PAGE_SIZE = 128
DEC_SEQ = 4
NEG = -1e30

LANES = 128
MXU_DIM = 256
VMEM_LIMIT = 52 * 1024 * 1024

BIAS_TABLE_LEN = 8320


def _cparams(sem):
    return pltpu.CompilerParams(dimension_semantics=sem, vmem_limit_bytes=VMEM_LIMIT)


def _nt_dot(a, b):
    return lax.dot_general(a, b, (((1,), (1,)), ((), ())), preferred_element_type=F32)


def _dot(a, b):
    return jnp.dot(a, b, preferred_element_type=F32)


def _split_bf16(x):
    hi = x.astype(BF16)
    lo = (x - hi.astype(F32)).astype(BF16)
    return hi, lo


def _positions_minor(cache):
    n, pool, pos, heads, dim = cache.shape
    return jnp.transpose(cache, (0, 1, 3, 4, 2)).reshape(n * pool, heads * dim, pos)


def _bucket_thresholds():
    max_exact = NUM_BUCKETS // 2
    n = np.arange(1, 1 << 14, dtype=np.float64)
    large = max_exact + (np.log(n / max_exact) / math.log(REL_MAX_DIST / max_exact)
                         * (NUM_BUCKETS - max_exact)).astype(np.int64)
    return tuple(int(n[large >= max_exact + k][0]) for k in range(1, NUM_BUCKETS - max_exact))


_BUCKET_THRESHOLDS = _bucket_thresholds()


def _bias_table_kernel(bt_ref, o_ref):
    max_exact = NUM_BUCKETS // 2
    n = lax.broadcasted_iota(jnp.int32, (N_HEADS, BIAS_TABLE_LEN), 1)
    large = jnp.full(n.shape, max_exact, jnp.int32)
    for thr in _BUCKET_THRESHOLDS:
        large = large + (n >= thr).astype(jnp.int32)
    bucket = jnp.where(n < max_exact, n, jnp.minimum(large, NUM_BUCKETS - 1))
    bt = bt_ref[...]
    acc = jnp.zeros(n.shape, F32)
    for b in range(NUM_BUCKETS):
        acc = jnp.where(bucket == b, jnp.broadcast_to(bt[:, b:b + 1], n.shape), acc)
    o_ref[...] = acc


def _bias_table(rel_bias):
    return pl.pallas_call(
        _bias_table_kernel,
        out_shape=jax.ShapeDtypeStruct((N_HEADS, BIAS_TABLE_LEN), F32),
        name="bias_table",
    )(rel_bias.T)


def _toeplitz(v, n_rows, n_cols):
    h = v.shape[0]
    p = n_rows + n_cols + 1
    pad = jnp.zeros((h, 1), v.dtype)
    w = jnp.concatenate([pad, v[:, ::-1], pad], axis=1)
    x = jnp.tile(w[:, None, :], (1, n_rows, 1)).reshape(h, n_rows * p)
    x = x[:, :n_rows * (p - 1)].reshape(h, n_rows, p - 1)
    return x[:, :, n_rows:n_rows + n_cols]


PROJ_TN = 512


def _group_sumsq(acc, bd, axis):
    hi, lo = _split_bf16(acc * acc)
    parts = []
    for c in range(PROJ_TN // MXU_DIM):
        sl = slice(c * MXU_DIM, (c + 1) * MXU_DIM)
        if axis == 1:
            parts.append(_dot(hi[:, sl], bd) + _dot(lo[:, sl], bd))
        else:
            parts.append(_dot(bd, hi[sl, :]) + _dot(bd, lo[sl, :]))
    return jnp.concatenate(parts, axis=axis)


def _proj_in_kernel(*refs, groups, transposed):
    n_w = len(groups)
    n_g = sum(1 for gs in groups if gs)
    x_ref, g_ref = refs[0], refs[1]
    w_refs = refs[2:2 + n_w]
    gain_refs = list(refs[2 + n_w:2 + n_w + n_g])
    bd_ref = refs[2 + n_w + n_g]
    out_refs = refs[3 + n_w + n_g:3 + 2 * n_w + n_g]
    h_scr = refs[3 + 2 * n_w + n_g]

    @pl.when(pl.program_id(1) == 0)
    def _():
        x = x_ref[...]
        ms = jnp.mean(x * x, axis=-1, keepdims=True)
        h_scr[...] = (x * lax.rsqrt(ms + EPS) * g_ref[...]).astype(BF16)

    h = h_scr[...]
    for k in range(n_w):
        gs = groups[k]
        if transposed[k]:
            acc = _nt_dot(w_refs[k][...], h)
        else:
            acc = _dot(h, w_refs[k][...])
        if gs:
            bd = bd_ref[0 if gs == HEAD_DIM else 1]
            ss = _group_sumsq(acc, bd, 0 if transposed[k] else 1)
            acc = acc * lax.rsqrt(ss * (1.0 / gs) + EPS) * gain_refs.pop(0)[...]
        if transposed[k]:
            out_refs[k][0] = acc
        else:
            out_refs[k][...] = acc


def _block_diag_ones():
    r = np.arange(MXU_DIM)
    b64 = (r[:, None] // HEAD_DIM == r[None, :] // HEAD_DIM)
    b32 = (r[:, None] // DIFF_DIM == r[None, :] // DIFF_DIM)
    return jnp.asarray(np.stack([b64, b32]).astype(np.float32), dtype=BF16)


def _proj_in(x, g, weights, gains, groups, transposed, tm, seq):
    m, d = x.shape
    tn = PROJ_TN
    n_w = len(weights)
    w_dim = weights[0].shape[0] if transposed[0] else weights[0].shape[1]
    spb = seq // tm
    in_specs = [pl.BlockSpec((tm, d), lambda i, j: (i, 0)),
                pl.BlockSpec((1, d), lambda i, j: (0, 0))]
    for k in range(n_w):
        if transposed[k]:
            in_specs.append(pl.BlockSpec((tn, d), lambda i, j: (j, 0)))
        else:
            in_specs.append(pl.BlockSpec((d, tn), lambda i, j: (0, j)))
    for k in range(n_w):
        if groups[k]:
            if transposed[k]:
                in_specs.append(pl.BlockSpec((tn, 1), lambda i, j: (j, 0)))
            else:
                in_specs.append(pl.BlockSpec((1, tn), lambda i, j: (0, j)))
    in_specs.append(pl.BlockSpec((2, MXU_DIM, MXU_DIM), lambda i, j: (0, 0, 0)))
    out_specs, out_shape = [], []
    for k in range(n_w):
        if transposed[k]:
            out_specs.append(pl.BlockSpec((1, tn, tm), lambda i, j: (i // spb, j, i % spb)))
            out_shape.append(jax.ShapeDtypeStruct((m // seq, w_dim, seq), F32))
        else:
            out_specs.append(pl.BlockSpec((tm, tn), lambda i, j: (i, j)))
            out_shape.append(jax.ShapeDtypeStruct((m, w_dim), F32))
    return pl.pallas_call(
        functools.partial(_proj_in_kernel, groups=tuple(groups), transposed=tuple(transposed)),
        grid=(m // tm, w_dim // tn),
        in_specs=in_specs,
        out_specs=out_specs,
        out_shape=out_shape,
        scratch_shapes=[pltpu.VMEM((tm, d), BF16)],
        compiler_params=_cparams(("parallel", "arbitrary")),
        name="proj_in",
    )(x, g, *weights, *gains, _block_diag_ones())


def _proj_out_kernel(*refs, merge):
    if merge:
        o0, o1, o2, l0, l1, l2, gate_ref, x_ref, w_ref, out_ref = refs
        la, lb, lc = l0[...], l1[...], l2[...]
        m = jnp.maximum(jnp.maximum(la, lb), lc)
        ea, eb, ec = jnp.exp(la - m), jnp.exp(lb - m), jnp.exp(lc - m)
        o = (ea * o0[...] + eb * o1[...] + ec * o2[...]) / (ea + eb + ec)
    else:
        o_ref, gate_ref, x_ref, w_ref, out_ref = refs
        o = o_ref[...]
    gate = gate_ref[...]
    act = o * (gate / (1.0 + jnp.exp(-gate)))
    out_ref[...] = x_ref[...] + _dot(act.astype(BF16), w_ref[...])


def _proj_out(o_list, gate, x, w, tm):
    m, wd = gate.shape
    d = x.shape[1]
    merge = len(o_list) > 1
    row = lambda i: (i, 0)
    in_specs = [pl.BlockSpec((tm, wd), row) for _ in o_list]
    in_specs += [pl.BlockSpec((tm, wd), row), pl.BlockSpec((tm, d), row),
                 pl.BlockSpec((wd, d), lambda i: (0, 0))]
    return pl.pallas_call(
        functools.partial(_proj_out_kernel, merge=merge),
        grid=(m // tm,),
        in_specs=in_specs,
        out_specs=pl.BlockSpec((tm, d), row),
        out_shape=jax.ShapeDtypeStruct((m, d), F32),
        compiler_params=_cparams(("parallel",)),
        name="proj_out",
    )(*o_list, gate, x, w)


A_BLK = 128


def _dil_prompt_kernel(*refs, use_prev):
    if use_prev:
        q_ref, kc_ref, vc_ref, kp_ref, vp_ref, t_ref, o_ref, l_ref = refs
    else:
        q_ref, kc_ref, vc_ref, t_ref, o_ref, l_ref = refs
    scale = HEAD_DIM ** -0.5
    for h in range(H_GROUP):
        sl = slice(HEAD_DIM * h, HEAD_DIM * (h + 1))
        qh = q_ref[0, :, sl].astype(BF16)
        s = _nt_dot(qh, kc_ref[0, :, sl].astype(BF16)) * scale + t_ref[0, h, :, A_BLK:]
        m = jnp.max(s, axis=1, keepdims=True)
        if use_prev:
            sp = _nt_dot(qh, kp_ref[0, :, sl].astype(BF16)) * scale + t_ref[0, h, :, :A_BLK]
            m = jnp.maximum(m, jnp.max(sp, axis=1, keepdims=True))
        e = jnp.exp(s - m)
        l = jnp.sum(e, axis=1, keepdims=True)
        o = _dot(e.astype(BF16), vc_ref[0, :, sl].astype(BF16))
        if use_prev:
            ep = jnp.exp(sp - m)
            l = l + jnp.sum(ep, axis=1, keepdims=True)
            o = o + _dot(ep.astype(BF16), vp_ref[0, :, sl].astype(BF16))
        o_ref[0, :, sl] = o / l
        l_ref[0, :, sl] = jnp.broadcast_to(m + jnp.log(l), (A_BLK, HEAD_DIM))


def _group_bias(bias_tab, g):
    window, dil = DILATED_CONFIGS[g]
    n_keys = window // dil + 1
    return bias_tab[g * H_GROUP:(g + 1) * H_GROUP, 0:n_keys * dil:dil]


def _dil_prompt_tables(bias_tab):
    tables = []
    for g in range(len(DILATED_CONFIGS)):
        bj = _group_bias(bias_tab, g)
        lo, hi = A_BLK - (2 * A_BLK - 1), A_BLK + A_BLK - 1
        v = jnp.full((H_GROUP, hi - lo + 1), NEG, F32)
        v = lax.dynamic_update_slice(v, bj, (0, -lo))
        t_gen = _toeplitz(v, A_BLK, 2 * A_BLK)
        t_first = t_gen.at[:, :, :A_BLK].set(NEG)
        tables.append(jnp.stack([t_first, t_gen]))
    return tables


def _dil_prompt_group(q, k, v, table, g, batch, seq):
    dil = DILATED_CONFIGS[g][1]
    u = seq // dil
    nqb = u // A_BLK
    use_prev = nqb > 1
    n_col = WIDTH // A_OUT
    qv = q.reshape(batch, u, dil * WIDTH)
    kv = k.reshape(batch, u, dil * WIDTH)
    vv = v.reshape(batch, u, dil * WIDTH)
    cur = lambda b, r, i: (b, i, r * n_col + g)
    prev = lambda b, r, i: (b, jnp.maximum(i - 1, 0), r * n_col + g)
    blk = (1, A_BLK, A_OUT)
    in_specs = [pl.BlockSpec(blk, cur), pl.BlockSpec(blk, cur), pl.BlockSpec(blk, cur)]
    args = [qv, kv, vv]
    if use_prev:
        in_specs += [pl.BlockSpec(blk, prev), pl.BlockSpec(blk, prev)]
        args += [kv, vv]
    in_specs += [pl.BlockSpec((1, H_GROUP, A_BLK, 2 * A_BLK), lambda b, r, i: (jnp.minimum(i, 1), 0, 0, 0))]
    args += [table]
    out_map = lambda b, r, i: (b, i, r)
    o, lse = pl.pallas_call(
        functools.partial(_dil_prompt_kernel, use_prev=use_prev),
        grid=(batch, dil, nqb),
        in_specs=in_specs,
        out_specs=[pl.BlockSpec(blk, out_map), pl.BlockSpec(blk, out_map)],
        out_shape=[jax.ShapeDtypeStruct((batch, u, dil * A_OUT), F32)] * 2,
        compiler_params=_cparams(("parallel", "parallel", "arbitrary")),
        name="dilated_prompt_g%d" % g,
    )(*args)
    return o.reshape(batch * seq, A_OUT), lse.reshape(batch * seq, A_OUT)


NEW_PAD = 16
A_ROWS = H_GROUP * DEC_SEQ


def _dil_sample_kernel(qbd_ref, kn_ref, vn_ref, k0, v0, k1, v1, k2, v2, tp0, tp1, tp2, tn_ref, o_ref):
    scale = HEAD_DIM ** -0.5
    outs, lses = [], []
    for g, (kr, vr, tp) in enumerate(((k0, v0, tp0), (k1, v1, tp1), (k2, v2, tp2))):
        gs = slice(A_OUT * g, A_OUT * (g + 1))
        qb = qbd_ref[0, g]
        s = _dot(qb, kr[0].astype(BF16)) * scale + tp[...]
        sn = _nt_dot(qb, kn_ref[0, :, gs].astype(BF16)) * scale + tn_ref[g]
        m = jnp.maximum(jnp.max(s, axis=1, keepdims=True), jnp.max(sn, axis=1, keepdims=True))
        e = jnp.exp(s - m)
        en = jnp.exp(sn - m)
        l = jnp.sum(e, axis=1, keepdims=True) + jnp.sum(en, axis=1, keepdims=True)
        o = _nt_dot(e.astype(BF16), vr[0].astype(BF16)) + _dot(en.astype(BF16), vn_ref[0, :, gs].astype(BF16))
        outs.append(o / l)
        lses.append(m + jnp.log(l))
    mm = jnp.maximum(jnp.maximum(lses[0], lses[1]), lses[2])
    es = [jnp.exp(x - mm) for x in lses]
    o_ref[0] = (es[0] * outs[0] + es[1] * outs[1] + es[2] * outs[2]) / (es[0] + es[1] + es[2])


def _dil_sample_tables(bias_tab):
    tps, tns = [], []
    tq = np.arange(DEC_SEQ)
    for g, (window, dil) in enumerate(DILATED_CONFIGS):
        bg = bias_tab[g * H_GROUP:(g + 1) * H_GROUP]
        length = window
        pos = np.arange(length)
        dist = length + tq[:, None] - pos[None, :]
        ok = (dist % dil == 0) & (dist <= window)
        rows = jnp.stack([bg[:, t + 1:t + 1 + length][:, ::-1] for t in range(DEC_SEQ)], axis=1)
        tps.append(jnp.where(jnp.asarray(ok)[None], rows, NEG).reshape(A_ROWS, length))
        tn_ = np.arange(NEW_PAD)
        dn = tq[:, None] - tn_[None, :]
        okn = (dn >= 0) & (tn_[None, :] < DEC_SEQ) & (dn % dil == 0)
        vals = bg[:, np.clip(dn, 0, DEC_SEQ)]
        tns.append(jnp.where(jnp.asarray(okn)[None], vals, NEG).reshape(A_ROWS, NEW_PAD))
    return tps, jnp.stack(tns)


def _pad_new(x, dec_batch, rows):
    x = x.reshape(dec_batch, DEC_SEQ, x.shape[-1])
    return jnp.pad(x, ((0, 0), (0, rows - DEC_SEQ), (0, 0)))


def _dil_sample(q, k, v, caches_k, caches_v, j, tps, tn, dec_batch):
    n_grp = len(DILATED_CONFIGS)
    sel = np.zeros((H_GROUP, DEC_SEQ, A_ROWS), np.float32)
    for h in range(H_GROUP):
        for t in range(DEC_SEQ):
            sel[h, t, h * DEC_SEQ + t] = 1.0
    q5 = q.reshape(dec_batch, DEC_SEQ, n_grp, H_GROUP, HEAD_DIM)
    qbd = jnp.einsum("btghd,htr->bgrhd", q5, jnp.asarray(sel), precision=lax.Precision.HIGHEST)
    qbd = qbd.reshape(dec_batch, n_grp, A_ROWS, A_OUT).astype(BF16)
    kn, vn = (_pad_new(a, dec_batch, NEW_PAD) for a in (k, v))
    args = [qbd, kn, vn]
    in_specs = [pl.BlockSpec((1, n_grp, A_ROWS, A_OUT), lambda b: (b, 0, 0, 0)),
                pl.BlockSpec((1, NEW_PAD, WIDTH), lambda b: (b, 0, 0)),
                pl.BlockSpec((1, NEW_PAD, WIDTH), lambda b: (b, 0, 0))]
    for g, (window, dil) in enumerate(DILATED_CONFIGS):
        for c in (caches_k[g], caches_v[g]):
            args.append(_positions_minor(c))
            in_specs.append(pl.BlockSpec((1, A_OUT, window), lambda b, j=j: (j * dec_batch + b, 0, 0)))
    for tp in tps:
        args.append(tp)
        in_specs.append(pl.BlockSpec(tp.shape, lambda b: (0, 0)))
    args.append(tn)
    in_specs.append(pl.BlockSpec(tn.shape, lambda b: (0, 0, 0)))
    o = pl.pallas_call(
        _dil_sample_kernel,
        grid=(dec_batch,),
        in_specs=in_specs,
        out_specs=pl.BlockSpec((1, A_ROWS, A_OUT), lambda b: (b, 0, 0)),
        out_shape=jax.ShapeDtypeStruct((dec_batch, A_ROWS, A_OUT), F32),
        compiler_params=_cparams(("parallel",)),
        name="dilated_sample",
    )(*args)
    o = o.reshape(dec_batch, H_GROUP, DEC_SEQ, H_GROUP, HEAD_DIM)
    idx = np.arange(H_GROUP)
    o = o[:, idx, :, idx, :]
    return jnp.transpose(o, (1, 2, 0, 3)).reshape(dec_batch * DEC_SEQ, A_OUT)


BC_BLK = 256
HEADS_PER_STEP = LANES // HEAD_DIM


def _lambda_value(lam_ref, lambda_init):
    lv = lam_ref[...]
    s1 = jnp.sum(lv[0:1] * lv[1:2], axis=1, keepdims=True)
    s2 = jnp.sum(lv[2:3] * lv[3:4], axis=1, keepdims=True)
    return jnp.exp(s1) - jnp.exp(s2) + lambda_init


def _diff_prompt_kernel(q_ref, kt_ref, vt_ref, t_ref, lam_ref, sg_ref, o_ref, *, lambda_init):
    qi = pl.program_id(2)
    scale = DIFF_DIM ** -0.5
    lam = _lambda_value(lam_ref, lambda_init)
    for hl in range(HEADS_PER_STEP):
        vsl = slice(HEAD_DIM * hl, HEAD_DIM * (hl + 1))
        halves = []
        for half in range(2):
            c0 = HEAD_DIM * hl + DIFF_DIM * half
            qh = q_ref[0, :, c0:c0 + DIFF_DIM].astype(BF16)

            def body(kb, carry, c0=c0, qh=qh, hl=hl, vsl=vsl):
                m, l, acc = carry
                cols = pl.ds(pl.multiple_of(kb * BC_BLK, BC_BLK), BC_BLK)
                ks = kt_ref[0, c0:c0 + DIFF_DIM, cols].astype(BF16)
                s = _dot(qh, ks) * scale + t_ref[hl, qi - kb]
                m_new = jnp.maximum(m, jnp.max(s, axis=1, keepdims=True))
                alpha = jnp.exp(m - m_new)
                p = jnp.exp(s - m_new)
                l = alpha * l + jnp.sum(p, axis=1, keepdims=True)
                acc = alpha * acc + _nt_dot(p.astype(BF16), vt_ref[0, vsl, cols].astype(BF16))
                return m_new, l, acc

            init = (jnp.full((BC_BLK, 1), NEG, F32), jnp.zeros((BC_BLK, 1), F32),
                    jnp.zeros((BC_BLK, HEAD_DIM), F32))
            _, l, acc = lax.fori_loop(0, qi + 1, body, init)
            halves.append(acc / l)
        o = halves[0] - lam * halves[1]
        ms = jnp.mean(o * o, axis=-1, keepdims=True)
        o_ref[0, :, vsl] = o * lax.rsqrt(ms + EPS) * sg_ref[...] * (1.0 - lambda_init)


def _diff_prompt_tables(bias_tab, seq):
    nblk = seq // BC_BLK
    ext = jnp.concatenate([jnp.full((N_HEADS, BC_BLK), NEG, F32), bias_tab], axis=1)
    tiles = []
    for dd in range(nblk):
        c = dd * BC_BLK
        lo = c - (BC_BLK - 1)
        v = ext[:, lo + BC_BLK: lo + BC_BLK + 2 * BC_BLK - 1]
        tiles.append(_toeplitz(v, BC_BLK, BC_BLK))
    return jnp.stack(tiles, axis=1)


def _diff_prompt(q, kt, vt, table, lam_vecs, sub_gain, lambda_init, batch, seq):
    nqb = seq // BC_BLK
    n_hg = WIDTH // LANES
    o = pl.pallas_call(
        functools.partial(_diff_prompt_kernel, lambda_init=lambda_init),
        grid=(n_hg, batch, nqb),
        in_specs=[pl.BlockSpec((1, BC_BLK, LANES), lambda h, b, i: (b, i, h)),
                  pl.BlockSpec((1, LANES, seq), lambda h, b, i: (b, h, 0)),
                  pl.BlockSpec((1, LANES, seq), lambda h, b, i: (b, h, 0)),
                  pl.BlockSpec((HEADS_PER_STEP, nqb, BC_BLK, BC_BLK), lambda h, b, i: (h, 0, 0, 0)),
                  pl.BlockSpec((4, DIFF_DIM), lambda h, b, i: (0, 0)),
                  pl.BlockSpec((1, HEAD_DIM), lambda h, b, i: (0, 0))],
        out_specs=pl.BlockSpec((1, BC_BLK, LANES), lambda h, b, i: (b, i, h)),
        out_shape=jax.ShapeDtypeStruct((batch, seq, WIDTH), F32),
        compiler_params=_cparams(("parallel", "parallel", "arbitrary")),
        name="diff_prompt",
    )(q.reshape(batch, seq, WIDTH), kt, vt, table, lam_vecs, sub_gain)
    return o.reshape(batch * seq, WIDTH)


def _log_sigmoid(z):
    return -(jnp.maximum(-z, 0.0) + jnp.log1p(jnp.exp(-jnp.abs(z))))


def _suffix_sum_rows(x, tri):
    hi, lo = _split_bf16(x)
    return _dot(hi, tri) + _dot(lo, tri)


def _stick_prompt_kernel(q_ref, kt_ref, vt_ref, tri_ref, o_ref):
    qi = pl.program_id(2)
    scale = HEAD_DIM ** -0.5
    tri = tri_ref[...]
    row = lax.broadcasted_iota(jnp.int32, (BC_BLK, BC_BLK), 0)
    col = lax.broadcasted_iota(jnp.int32, (BC_BLK, BC_BLK), 1)
    strict = col < row
    for hl in range(HEADS_PER_STEP):
        sl = slice(HEAD_DIM * hl, HEAD_DIM * (hl + 1))
        qh = q_ref[0, :, sl].astype(BF16)

        cols = pl.ds(pl.multiple_of(qi * BC_BLK, BC_BLK), BC_BLK)
        z = _dot(qh, kt_ref[0, sl, cols].astype(BF16)) * scale
        ls = _log_sigmoid(z)
        stay = jnp.where(strict, ls - z, 0.0)
        after = _suffix_sum_rows(stay, tri)
        a = jnp.where(strict, jnp.exp(ls + after), 0.0)
        acc = _nt_dot(a.astype(BF16), vt_ref[0, sl, cols].astype(BF16))
        c = jnp.sum(stay, axis=1, keepdims=True)

        def body(step, carry, qh=qh, sl=sl):
            c, acc = carry
            kb = qi - 1 - step
            cols = pl.ds(pl.multiple_of(kb * BC_BLK, BC_BLK), BC_BLK)
            z = _dot(qh, kt_ref[0, sl, cols].astype(BF16)) * scale
            ls = _log_sigmoid(z)
            stay = ls - z
            after = _suffix_sum_rows(stay, tri)
            a = jnp.exp(ls + after + c)
            acc = acc + _nt_dot(a.astype(BF16), vt_ref[0, sl, cols].astype(BF16))
            return c + jnp.sum(stay, axis=1, keepdims=True), acc

        _, acc = lax.fori_loop(0, qi, body, (c, acc))
        o_ref[0, :, sl] = acc


def _tri_later(n):
    r = np.arange(n)
    return jnp.asarray((r[:, None] > r[None, :]).astype(np.float32), dtype=BF16)


def _stick_prompt(q, kt, vt, batch, seq):
    nqb = seq // BC_BLK
    n_hg = WIDTH // LANES
    o = pl.pallas_call(
        _stick_prompt_kernel,
        grid=(n_hg, batch, nqb),
        in_specs=[pl.BlockSpec((1, BC_BLK, LANES), lambda h, b, i: (b, i, h)),
                  pl.BlockSpec((1, LANES, seq), lambda h, b, i: (b, h, 0)),
                  pl.BlockSpec((1, LANES, seq), lambda h, b, i: (b, h, 0)),
                  pl.BlockSpec((BC_BLK, BC_BLK), lambda h, b, i: (0, 0))],
        out_specs=pl.BlockSpec((1, BC_BLK, LANES), lambda h, b, i: (b, i, h)),
        out_shape=jax.ShapeDtypeStruct((batch, seq, WIDTH), F32),
        compiler_params=_cparams(("parallel", "parallel", "arbitrary")),
        name="stick_prompt",
    )(q.reshape(batch, seq, WIDTH), kt, vt, _tri_later(BC_BLK))
    return o.reshape(batch * seq, WIDTH)


HEADS_PER_GROUP = MXU_DIM // HEAD_DIM
N_VGROUPS = WIDTH // MXU_DIM
B_QROWS = 256
C_QROWS = 128
B_ROWS = 2 * HEADS_PER_GROUP * DEC_SEQ
C_ROWS = HEADS_PER_GROUP * DEC_SEQ


def _diff_sample_kernel(pt_ref, qbd_ref, kc_ref, vc_ref, kn_ref, vn_ref, bias_ref, lam_ref, sg_ref,
                        o_ref, s_scr, m_scr, l_scr, acc_scr, *, lambda_init, n_pages):
    ph = pl.program_id(1)
    p = pl.program_id(2)
    scale = DIFF_DIM ** -0.5

    def score_page(k_page):
        s = _dot(qbd_ref[0], k_page.astype(BF16)) * scale + bias_ref[...]
        s_scr[p] = s
        m_scr[...] = jnp.maximum(m_scr[...], s)

    def value_page(v_page):
        e = jnp.exp(s_scr[p] - m_scr[...])
        l_scr[...] += e
        v16 = v_page.astype(BF16)
        for c in range(N_VGROUPS):
            acc_scr[c] += _nt_dot(e[B_ROWS * c:B_ROWS * (c + 1), :].astype(BF16),
                                  v16[MXU_DIM * c:MXU_DIM * (c + 1), :])

    @pl.when((ph == 0) & (p == 0))
    def _():
        m_scr[...] = jnp.full(m_scr.shape, NEG, F32)
        l_scr[...] = jnp.zeros(l_scr.shape, F32)
        acc_scr[...] = jnp.zeros(acc_scr.shape, F32)

    @pl.when((ph == 0) & (p < n_pages))
    def _():
        score_page(kc_ref[0])

    @pl.when((ph == 0) & (p == n_pages))
    def _():
        score_page(kn_ref[0])

    @pl.when((ph == 1) & (p == 0))
    def _():
        m_scr[...] = jnp.broadcast_to(jnp.max(m_scr[...], axis=1, keepdims=True), m_scr.shape)

    @pl.when((ph == 1) & (p < n_pages))
    def _():
        value_page(vc_ref[0])

    @pl.when((ph == 1) & (p == n_pages))
    def _():
        value_page(vn_ref[0])
        lam = _lambda_value(lam_ref, lambda_init)
        l = jnp.sum(l_scr[...], axis=1, keepdims=True)
        half_rows = B_ROWS // 2
        r = lax.broadcasted_iota(jnp.int32, (half_rows, MXU_DIM), 0)
        ln = lax.broadcasted_iota(jnp.int32, (half_rows, MXU_DIM), 1)
        own = (r // DEC_SEQ) == (ln // HEAD_DIM)
        for c in range(N_VGROUPS):
            an = acc_scr[c] / l[B_ROWS * c:B_ROWS * (c + 1)]
            d = jnp.where(own, an[:half_rows] - lam * an[half_rows:], 0.0)
            ms = jnp.sum(d * d, axis=1, keepdims=True) * (1.0 / HEAD_DIM)
            o_ref[0, c] = d * lax.rsqrt(ms + EPS) * sg_ref[...] * (1.0 - lambda_init)


def _diag_blocks(x, dec_batch):
    x = x.reshape(dec_batch, N_VGROUPS, HEADS_PER_GROUP, DEC_SEQ, HEADS_PER_GROUP, HEAD_DIM)
    idx = np.arange(HEADS_PER_GROUP)
    x = x[:, :, idx, :, idx, :]
    x = jnp.transpose(x, (1, 3, 2, 0, 4))
    return x.reshape(dec_batch * DEC_SEQ, WIDTH)


def _query_matrix(q, dec_batch, n_sub, sub_dim, n_rows, row_of):
    sel = np.zeros((n_sub, DEC_SEQ, n_rows), np.float32)
    for s in range(n_sub):
        for t in range(DEC_SEQ):
            sel[s, t, row_of(s, t)] = 1.0
    q4 = q.reshape(dec_batch, DEC_SEQ, n_sub, sub_dim)
    qbd = jnp.einsum("btsd,str->brsd", q4, jnp.asarray(sel), precision=lax.Precision.HIGHEST)
    return qbd.reshape(dec_batch, n_rows, WIDTH).astype(BF16)


def _b_row(hh, t):
    h, half = hh // 2, hh % 2
    c, hl = h // HEADS_PER_GROUP, h % HEADS_PER_GROUP
    return c * B_ROWS + half * (B_ROWS // 2) + hl * DEC_SEQ + t


def _c_row(h, t):
    c, hl = h // HEADS_PER_GROUP, h % HEADS_PER_GROUP
    return c * C_ROWS + hl * DEC_SEQ + t


def _diff_sample_bias(bias_tab, past_len):
    segs = [bias_tab[:, t + 1:t + 1 + past_len][:, ::-1] for t in range(DEC_SEQ)]
    past = jnp.stack(segs, axis=1)
    tn_ = np.arange(PAGE_SIZE)[None, :]
    tq = np.arange(DEC_SEQ)[:, None]
    dist = tq - tn_
    ok = (dist >= 0) & (tn_ < DEC_SEQ)
    new = jnp.where(jnp.asarray(ok)[None], bias_tab[:, np.clip(dist, 0, DEC_SEQ)], NEG)
    full = jnp.concatenate([past, new], axis=2)
    n_cols = past_len + PAGE_SIZE
    full = full.reshape(N_VGROUPS, 1, HEADS_PER_GROUP, DEC_SEQ, n_cols)
    full = jnp.broadcast_to(full, (N_VGROUPS, 2, HEADS_PER_GROUP, DEC_SEQ, n_cols))
    full = full.reshape(N_VGROUPS * B_ROWS, n_cols)
    return jnp.pad(full, ((0, B_QROWS - N_VGROUPS * B_ROWS), (0, 0)))


def _new_page(x, dec_batch):
    x = jnp.transpose(x.reshape(dec_batch, DEC_SEQ, WIDTH), (0, 2, 1))
    return jnp.pad(x, ((0, 0), (0, 0), (0, PAGE_SIZE - DEC_SEQ)))


def _diff_sample(q, k, v, cache_k, cache_v, j, page_table, bias_sb, lam_vecs, sub_gain, lambda_init, dec_batch):
    n_pool = cache_k.shape[1]
    n_pages = page_table.shape[1]
    ck = _positions_minor(cache_k)
    cv = _positions_minor(cache_v)
    qbd = _query_matrix(q, dec_batch, 2 * N_HEADS, DIFF_DIM, B_QROWS, _b_row)
    kn = _new_page(k, dec_batch)
    vn = _new_page(v, dec_batch)
    sg = jnp.tile(sub_gain, (1, HEADS_PER_GROUP))
    last = n_pages - 1

    def k_map(b, ph, p, pt):
        pp = jnp.where(ph == 0, jnp.minimum(p, last), last)
        return (j * n_pool + pt[b, pp], 0, 0)

    def v_map(b, ph, p, pt):
        pp = jnp.where(ph == 1, jnp.minimum(p, last), 0)
        return (j * n_pool + pt[b, pp], 0, 0)

    grid_spec = pltpu.PrefetchScalarGridSpec(
        num_scalar_prefetch=1,
        grid=(dec_batch, 2, n_pages + 1),
        in_specs=[pl.BlockSpec((1, B_QROWS, WIDTH), lambda b, ph, p, pt: (b, 0, 0)),
                  pl.BlockSpec((1, WIDTH, PAGE_SIZE), k_map),
                  pl.BlockSpec((1, WIDTH, PAGE_SIZE), v_map),
                  pl.BlockSpec((1, WIDTH, PAGE_SIZE), lambda b, ph, p, pt: (b, 0, 0)),
                  pl.BlockSpec((1, WIDTH, PAGE_SIZE), lambda b, ph, p, pt: (b, 0, 0)),
                  pl.BlockSpec((B_QROWS, PAGE_SIZE), lambda b, ph, p, pt: (0, jnp.where(ph == 0, p, n_pages))),
                  pl.BlockSpec((4, DIFF_DIM), lambda b, ph, p, pt: (0, 0)),
                  pl.BlockSpec((1, MXU_DIM), lambda b, ph, p, pt: (0, 0))],
        out_specs=pl.BlockSpec((1, N_VGROUPS, B_ROWS // 2, MXU_DIM), lambda b, ph, p, pt: (b, 0, 0, 0)),
        scratch_shapes=[pltpu.VMEM((n_pages + 1, B_QROWS, PAGE_SIZE), F32),
                        pltpu.VMEM((B_QROWS, PAGE_SIZE), F32),
                        pltpu.VMEM((B_QROWS, PAGE_SIZE), F32),
                        pltpu.VMEM((N_VGROUPS, B_ROWS, MXU_DIM), F32)])
    o = pl.pallas_call(
        functools.partial(_diff_sample_kernel, lambda_init=lambda_init, n_pages=n_pages),
        grid_spec=grid_spec,
        out_shape=jax.ShapeDtypeStruct((dec_batch, N_VGROUPS, B_ROWS // 2, MXU_DIM), F32),
        compiler_params=_cparams(("parallel", "arbitrary", "arbitrary")),
        name="diff_sample",
    )(page_table, qbd, ck, cv, kn, vn, bias_sb, lam_vecs, sg)
    return _diag_blocks(o, dec_batch)


def _stick_sample_kernel(pt_ref, qbd_ref, kc_ref, vc_ref, kn_ref, vn_ref, valid_ref, tri_ref,
                         o_ref, c_scr, acc_scr, *, n_pages):
    p = pl.program_id(1)
    scale = HEAD_DIM ** -0.5

    def page(k_page, v_page, valid):
        z = _dot(qbd_ref[0], k_page.astype(BF16)) * scale
        ls = _log_sigmoid(z)
        stay = ls - z
        if valid is not None:
            stay = stay * valid
        after = _suffix_sum_rows(stay, tri_ref[...])
        a = jnp.exp(ls + after + c_scr[...])
        if valid is not None:
            a = a * valid
        c_scr[...] += jnp.sum(stay, axis=1, keepdims=True)
        v16 = v_page.astype(BF16)
        for c in range(N_VGROUPS):
            acc_scr[c] += _nt_dot(a[C_ROWS * c:C_ROWS * (c + 1), :].astype(BF16),
                                  v16[MXU_DIM * c:MXU_DIM * (c + 1), :])

    @pl.when(p == 0)
    def _():
        c_scr[...] = jnp.zeros(c_scr.shape, F32)
        acc_scr[...] = jnp.zeros(acc_scr.shape, F32)
        page(kn_ref[0], vn_ref[0], valid_ref[...])

    @pl.when(p > 0)
    def _():
        page(kc_ref[0], vc_ref[0], None)

    @pl.when(p == n_pages)
    def _():
        o_ref[0] = acc_scr[...]


def _stick_sample(q, k, v, cache_k, cache_v, j, page_table, dec_batch):
    n_pool = cache_k.shape[1]
    n_pages = page_table.shape[1]
    ck = _positions_minor(cache_k)
    cv = _positions_minor(cache_v)
    qbd = _query_matrix(q, dec_batch, N_HEADS, HEAD_DIM, C_QROWS, _c_row)
    kn = _new_page(k, dec_batch)
    vn = _new_page(v, dec_batch)
    valid = np.zeros((C_QROWS, PAGE_SIZE), np.float32)
    for h in range(N_HEADS):
        for t in range(DEC_SEQ):
            valid[_c_row(h, t), :t] = 1.0

    def page_map(b, p, pt):
        return (j * n_pool + pt[b, jnp.clip(n_pages - p, 0, n_pages - 1)], 0, 0)

    grid_spec = pltpu.PrefetchScalarGridSpec(
        num_scalar_prefetch=1,
        grid=(dec_batch, n_pages + 1),
        in_specs=[pl.BlockSpec((1, C_QROWS, WIDTH), lambda b, p, pt: (b, 0, 0)),
                  pl.BlockSpec((1, WIDTH, PAGE_SIZE), page_map),
                  pl.BlockSpec((1, WIDTH, PAGE_SIZE), page_map),
                  pl.BlockSpec((1, WIDTH, PAGE_SIZE), lambda b, p, pt: (b, 0, 0)),
                  pl.BlockSpec((1, WIDTH, PAGE_SIZE), lambda b, p, pt: (b, 0, 0)),
                  pl.BlockSpec((C_QROWS, PAGE_SIZE), lambda b, p, pt: (0, 0)),
                  pl.BlockSpec((PAGE_SIZE, PAGE_SIZE), lambda b, p, pt: (0, 0))],
        out_specs=pl.BlockSpec((1, N_VGROUPS, C_ROWS, MXU_DIM), lambda b, p, pt: (b, 0, 0, 0)),
        scratch_shapes=[pltpu.VMEM((C_QROWS, 1), F32),
                        pltpu.VMEM((N_VGROUPS, C_ROWS, MXU_DIM), F32)])
    o = pl.pallas_call(
        functools.partial(_stick_sample_kernel, n_pages=n_pages),
        grid_spec=grid_spec,
        out_shape=jax.ShapeDtypeStruct((dec_batch, N_VGROUPS, C_ROWS, MXU_DIM), F32),
        compiler_params=_cparams(("parallel", "arbitrary")),
        name="stick_sample",
    )(page_table, qbd, ck, cv, kn, vn, jnp.asarray(valid), _tri_later(PAGE_SIZE))
    return _diag_blocks(o, dec_batch)


PROMPT_TM = 512


def _heads_last(xt, batch, seq, n_sub, sub_dim):
    return jnp.transpose(xt.reshape(batch, n_sub, sub_dim, seq), (0, 3, 1, 2))


def kernel(x_prompt, x_sample, cache_a_k_g0, cache_a_v_g0, cache_a_k_g1, cache_a_v_g1, cache_a_k_g2, cache_a_v_g2, cache_b_k, cache_b_v, cache_c_k, cache_c_v, page_table, norm_gain, rel_bias, w_in_a, w_out_a, q_gain_a, k_gain_a, w_in_b, w_out_b, q_gain_b, k_gain_b, lambda_q1, lambda_k1, lambda_q2, lambda_k2, sub_gain_b, w_in_c, w_out_c):
    batch, seq, _ = x_prompt.shape
    dec_batch = x_sample.shape[0]
    depth = norm_gain.shape[0]
    n_p = batch * seq
    n_s = dec_batch * DEC_SEQ
    caches_ak = (cache_a_k_g0, cache_a_k_g1, cache_a_k_g2)
    caches_av = (cache_a_v_g0, cache_a_v_g1, cache_a_v_g2)

    bias_tab = _bias_table(rel_bias)
    tabs_a = _dil_prompt_tables(bias_tab)
    tps_a, tn_a = _dil_sample_tables(bias_tab)
    tab_b = _diff_prompt_tables(bias_tab, seq)
    bias_sb = _diff_sample_bias(bias_tab, page_table.shape[1] * PAGE_SIZE)

    y_p = x_prompt.reshape(n_p, D_MODEL)
    y_s = x_sample.reshape(n_s, D_MODEL)
    a_kp = [[] for _ in DILATED_CONFIGS]
    a_vp = [[] for _ in DILATED_CONFIGS]
    a_ks, a_vs = [], []
    b_kp, b_vp, b_ks, b_vs = [], [], [], []
    c_kp, c_vp, c_ks, c_vs = [], [], [], []
    no_t = (False, False, False, False)

    for i in range(depth):
        kind, j = i % 3, i // 3
        g = norm_gain[i][None, :]
        if kind == 0:
            w = w_in_a[j].astype(BF16)
            wq, wk, wv, wg = (w[:, :WIDTH], w[:, WIDTH:2 * WIDTH], w[:, 2 * WIDTH:3 * WIDTH], w[:, 3 * WIDTH:])
            gq = jnp.tile(q_gain_a[j], N_HEADS)[None, :]
            gk = jnp.tile(k_gain_a[j], N_HEADS)[None, :]
            w_out = w_out_a[j].astype(BF16)
            res = []
            for y, tm, sq in ((y_p, PROMPT_TM, seq), (y_s, n_s, n_s)):
                q, k, v = _proj_in(y, g, [wq, wk, wv], [gq, gk], (HEAD_DIM, HEAD_DIM, 0), no_t[:3], tm, sq)
                gate, = _proj_in(y, g, [wg], [], (0,), no_t[:1], tm, sq)
                res.append((q, k, v, gate))
            q, k, v, gate = res[0]
            os_, ls_ = [], []
            for gi in range(len(DILATED_CONFIGS)):
                o_g, l_g = _dil_prompt_group(q, k, v, tabs_a[gi], gi, batch, seq)
                os_.append(o_g)
                ls_.append(l_g)
            y_p = _proj_out(os_ + ls_, gate, y_p, w_out, PROMPT_TM)
            k4 = k.reshape(batch, seq, N_HEADS, HEAD_DIM)
            v4 = v.reshape(batch, seq, N_HEADS, HEAD_DIM)
            for gi, (window, _) in enumerate(DILATED_CONFIGS):
                keep = min(window, seq)
                hs = slice(gi * H_GROUP, (gi + 1) * H_GROUP)
                a_kp[gi].append(k4[:, seq - keep:, hs])
                a_vp[gi].append(v4[:, seq - keep:, hs])
            q, k, v, gate = res[1]
            o = _dil_sample(q, k, v, caches_ak, caches_av, j, tps_a, tn_a, dec_batch)
            y_s = _proj_out([o], gate, y_s, w_out, n_s)
            a_ks.append(k.reshape(dec_batch, DEC_SEQ, N_HEADS, HEAD_DIM))
            a_vs.append(v.reshape(dec_batch, DEC_SEQ, N_HEADS, HEAD_DIM))
        else:
            w_in = (w_in_b if kind == 1 else w_in_c)[j].astype(BF16)
            w_out = (w_out_b if kind == 1 else w_out_c)[j].astype(BF16)
            ws = [w_in[:, c * WIDTH:(c + 1) * WIDTH] for c in range(4)]
            ws_p = [ws[0], ws[1].T, ws[2].T, ws[3]]
            tr_p = (False, True, True, False)
            if kind == 1:
                lambda_init = float(0.8 - 0.6 * math.exp(-0.3 * i))
                gq = jnp.tile(q_gain_b[j], 2 * N_HEADS)[None, :]
                gk = jnp.tile(k_gain_b[j], 2 * N_HEADS)[None, :]
                groups = (DIFF_DIM, DIFF_DIM, 0, 0)
                gains_p, gains_s = [gq, gk.T], [gq, gk]
                lam_vecs = jnp.stack([lambda_q1[j], lambda_k1[j], lambda_q2[j], lambda_k2[j]])
                sub_gain = sub_gain_b[j][None, :]
            else:
                groups = (0, 0, 0, 0)
                gains_p, gains_s = [], []
            q, kt, vt, gate = _proj_in(y_p, g, ws_p, gains_p, groups, tr_p, PROMPT_TM, seq)
            qs, ks_, vs_, gate_s = _proj_in(y_s, g, ws, gains_s, groups, no_t, n_s, n_s)
            if kind == 1:
                o = _diff_prompt(q, kt, vt, tab_b, lam_vecs, sub_gain, lambda_init, batch, seq)
                o_s = _diff_sample(qs, ks_, vs_, cache_b_k, cache_b_v, j, page_table, bias_sb, lam_vecs,
                                   sub_gain, lambda_init, dec_batch)
                b_kp.append(_heads_last(kt, batch, seq, 2 * N_HEADS, DIFF_DIM))
                b_vp.append(_heads_last(vt, batch, seq, N_HEADS, HEAD_DIM))
                b_ks.append(ks_.reshape(dec_batch, DEC_SEQ, 2 * N_HEADS, DIFF_DIM))
                b_vs.append(vs_.reshape(dec_batch, DEC_SEQ, N_HEADS, HEAD_DIM))
            else:
                o = _stick_prompt(q, kt, vt, batch, seq)
                o_s = _stick_sample(qs, ks_, vs_, cache_c_k, cache_c_v, j, page_table, dec_batch)
                c_kp.append(_heads_last(kt, batch, seq, N_HEADS, HEAD_DIM))
                c_vp.append(_heads_last(vt, batch, seq, N_HEADS, HEAD_DIM))
                c_ks.append(ks_.reshape(dec_batch, DEC_SEQ, N_HEADS, HEAD_DIM))
                c_vs.append(vs_.reshape(dec_batch, DEC_SEQ, N_HEADS, HEAD_DIM))
            y_p = _proj_out([o], gate, y_p, w_out, PROMPT_TM)
            y_s = _proj_out([o_s], gate_s, y_s, w_out, n_s)

    return (y_p.reshape(batch, seq, D_MODEL), y_s.reshape(dec_batch, DEC_SEQ, D_MODEL),
            jnp.stack(a_kp[0]), jnp.stack(a_vp[0]), jnp.stack(a_kp[1]), jnp.stack(a_vp[1]),
            jnp.stack(a_kp[2]), jnp.stack(a_vp[2]), jnp.stack(a_ks), jnp.stack(a_vs),
            jnp.stack(b_kp), jnp.stack(b_vp), jnp.stack(b_ks), jnp.stack(b_vs),
            jnp.stack(c_kp), jnp.stack(c_vp), jnp.stack(c_ks), jnp.stack(c_vs))
```

```python
import functools
import math

import numpy as np
import jax
import jax.numpy as jnp
from jax import lax
from jax.experimental import pallas as pl
from jax.experimental.pallas import tpu as pltpu

F32 = jnp.float32
BF16 = jnp.bfloat16

D_MODEL = 1024
N_HEADS = 24
HEAD_DIM = 64
WIDTH = N_HEADS * HEAD_DIM
H_GROUP = 8
A_OUT = H_GROUP * HEAD_DIM
DIFF_DIM = HEAD_DIM // 2
DILATED_CONFIGS = ((128, 1), (512, 4), (2048, 16))
NUM_BUCKETS = 32
REL_MAX_DIST = 2048
EPS = 1e-6
PAGE_SIZE = 128
DEC_SEQ = 4
NEG = -1e30

LANES = 128
MXU_DIM = 256
VMEM_LIMIT = 52 * 1024 * 1024

BIAS_TABLE_LEN = 8320


def _cparams(sem):
    return pltpu.CompilerParams(dimension_semantics=sem, vmem_limit_bytes=VMEM_LIMIT)


def _nt_dot(a, b):
    return lax.dot_general(a, b, (((1,), (1,)), ((), ())), preferred_element_type=F32)


def _dot(a, b):
    return jnp.dot(a, b, preferred_element_type=F32)


def _split_bf16(x):
    hi = x.astype(BF16)
    lo = (x - hi.astype(F32)).astype(BF16)
    return hi, lo


def _positions_minor(cache):
    n, pool, pos, heads, dim = cache.shape
    return jnp.transpose(cache, (0, 1, 3, 4, 2)).reshape(n * pool, heads * dim, pos)


def _bucket_thresholds():
    max_exact = NUM_BUCKETS // 2
    n = np.arange(1, 1 << 14, dtype=np.float64)
    large = max_exact + (np.log(n / max_exact) / math.log(REL_MAX_DIST / max_exact)
                         * (NUM_BUCKETS - max_exact)).astype(np.int64)
    return tuple(int(n[large >= max_exact + k][0]) for k in range(1, NUM_BUCKETS - max_exact))


_BUCKET_THRESHOLDS = _bucket_thresholds()


def _bias_table_kernel(bt_ref, o_ref):
    max_exact = NUM_BUCKETS // 2
    n = lax.broadcasted_iota(jnp.int32, (N_HEADS, BIAS_TABLE_LEN), 1)
    large = jnp.full(n.shape, max_exact, jnp.int32)
    for thr in _BUCKET_THRESHOLDS:
        large = large + (n >= thr).astype(jnp.int32)
    bucket = jnp.where(n < max_exact, n, jnp.minimum(large, NUM_BUCKETS - 1))
    bt = bt_ref[...]
    acc = jnp.zeros(n.shape, F32)
    for b in range(NUM_BUCKETS):
        acc = jnp.where(bucket == b, jnp.broadcast_to(bt[:, b:b + 1], n.shape), acc)
    o_ref[...] = acc


def _bias_table(rel_bias):
    return pl.pallas_call(
        _bias_table_kernel,
        out_shape=jax.ShapeDtypeStruct((N_HEADS, BIAS_TABLE_LEN), F32),
        name="bias_table",
    )(rel_bias.T)


def _toeplitz(v, n_rows, n_cols):
    h = v.shape[0]
    p = n_rows + n_cols + 1
    pad = jnp.zeros((h, 1), v.dtype)
    w = jnp.concatenate([pad, v[:, ::-1], pad], axis=1)
    x = jnp.tile(w[:, None, :], (1, n_rows, 1)).reshape(h, n_rows * p)
    x = x[:, :n_rows * (p - 1)].reshape(h, n_rows, p - 1)
    return x[:, :, n_rows:n_rows + n_cols]


PROJ_TN = 512


def _group_sumsq(acc, bd, axis):
    hi, lo = _split_bf16(acc * acc)
    parts = []
    for c in range(PROJ_TN // MXU_DIM):
        sl = slice(c * MXU_DIM, (c + 1) * MXU_DIM)
        if axis == 1:
            parts.append(_dot(hi[:, sl], bd) + _dot(lo[:, sl], bd))
        else:
            parts.append(_dot(bd, hi[sl, :]) + _dot(bd, lo[sl, :]))
    return jnp.concatenate(parts, axis=axis)


def _proj_in_kernel(*refs, groups, orient):
    n_w = len(groups)
    n_g = sum(1 for gs in groups if gs)
    x_ref, g_ref = refs[0], refs[1]
    w_refs = refs[2:2 + n_w]
    gain_refs = list(refs[2 + n_w:2 + n_w + n_g])
    bd_ref = refs[2 + n_w + n_g]
    out_refs = list(refs[3 + n_w + n_g:-1])
    h_scr = refs[-1]

    @pl.when(pl.program_id(1) == 0)
    def _():
        x = x_ref[...]
        ms = jnp.mean(x * x, axis=-1, keepdims=True)
        h_scr[...] = (x * lax.rsqrt(ms + EPS) * g_ref[...]).astype(BF16)

    h = h_scr[...]
    for k in range(n_w):
        gs = groups[k]
        if orient[k] == "t":
            acc = _nt_dot(w_refs[k][...], h)
        else:
            acc = _dot(h, w_refs[k][...])
        if gs:
            bd = bd_ref[0 if gs == HEAD_DIM else 1]
            ss = _group_sumsq(acc, bd, 0 if orient[k] == "t" else 1)
            acc = acc * lax.rsqrt(ss * (1.0 / gs) + EPS) * gain_refs.pop(0)[...]
        if orient[k] == "t":
            out_refs.pop(0)[0] = acc
        else:
            out_refs.pop(0)[...] = acc
            if orient[k] == "nt":
                out_refs.pop(0)[0] = acc.T


def _block_diag_ones():
    r = np.arange(MXU_DIM)
    b64 = (r[:, None] // HEAD_DIM == r[None, :] // HEAD_DIM)
    b32 = (r[:, None] // DIFF_DIM == r[None, :] // DIFF_DIM)
    return jnp.asarray(np.stack([b64, b32]).astype(np.float32), dtype=BF16)


def _proj_in(x, g, weights, gains, groups, orient, tm, seq):
    m, d = x.shape
    tn = PROJ_TN
    n_w = len(weights)
    w_dim = weights[0].shape[0] if orient[0] == "t" else weights[0].shape[1]
    spb = seq // tm
    in_specs = [pl.BlockSpec((tm, d), lambda i, j: (i, 0)),
                pl.BlockSpec((1, d), lambda i, j: (0, 0))]
    for k in range(n_w):
        if orient[k] == "t":
            in_specs.append(pl.BlockSpec((tn, d), lambda i, j: (j, 0)))
        else:
            in_specs.append(pl.BlockSpec((d, tn), lambda i, j: (0, j)))
    for k in range(n_w):
        if groups[k]:
            if orient[k] == "t":
                in_specs.append(pl.BlockSpec((tn, 1), lambda i, j: (j, 0)))
            else:
                in_specs.append(pl.BlockSpec((1, tn), lambda i, j: (0, j)))
    in_specs.append(pl.BlockSpec((2, MXU_DIM, MXU_DIM), lambda i, j: (0, 0, 0)))
    out_specs, out_shape = [], []
    for k in range(n_w):
        if "n" in orient[k]:
            out_specs.append(pl.BlockSpec((tm, tn), lambda i, j: (i, j)))
            out_shape.append(jax.ShapeDtypeStruct((m, w_dim), F32))
        if "t" in orient[k]:
            out_specs.append(pl.BlockSpec((1, tn, tm), lambda i, j: (i // spb, j, i % spb)))
            out_shape.append(jax.ShapeDtypeStruct((m // seq, w_dim, seq), F32))
    return pl.pallas_call(
        functools.partial(_proj_in_kernel, groups=tuple(groups), orient=tuple(orient)),
        grid=(m // tm, w_dim // tn),
        in_specs=in_specs,
        out_specs=out_specs,
        out_shape=out_shape,
        scratch_shapes=[pltpu.VMEM((tm, d), BF16)],
        compiler_params=_cparams(("parallel", "arbitrary")),
        name="proj_in",
    )(x, g, *weights, *gains, _block_diag_ones())


def _proj_out_kernel(*refs, merge):
    if merge:
        o0, o1, o2, l0, l1, l2, gate_ref, x_ref, w_ref, out_ref = refs
        la, lb, lc = l0[...], l1[...], l2[...]
        m = jnp.maximum(jnp.maximum(la, lb), lc)
        ea, eb, ec = jnp.exp(la - m), jnp.exp(lb - m), jnp.exp(lc - m)
        o = (ea * o0[...] + eb * o1[...] + ec * o2[...]) / (ea + eb + ec)
    else:
        o_ref, gate_ref, x_ref, w_ref, out_ref = refs
        o = o_ref[...]
    gate = gate_ref[...]
    act = o * (gate / (1.0 + jnp.exp(-gate)))
    out_ref[...] = x_ref[...] + _dot(act.astype(BF16), w_ref[...])


def _proj_out(o_list, gate, x, w, tm):
    m, wd = gate.shape
    d = x.shape[1]
    merge = len(o_list) > 1
    row = lambda i: (i, 0)
    in_specs = [pl.BlockSpec((tm, wd), row) for _ in o_list]
    in_specs += [pl.BlockSpec((tm, wd), row), pl.BlockSpec((tm, d), row),
                 pl.BlockSpec((wd, d), lambda i: (0, 0))]
    return pl.pallas_call(
        functools.partial(_proj_out_kernel, merge=merge),
        grid=(m // tm,),
        in_specs=in_specs,
        out_specs=pl.BlockSpec((tm, d), row),
        out_shape=jax.ShapeDtypeStruct((m, d), F32),
        compiler_params=_cparams(("parallel",)),
        name="proj_out",
    )(*o_list, gate, x, w)


A_BLK = 128


def _dil_prompt_kernel(*refs, use_prev):
    if use_prev:
        q_ref, kc_ref, vc_ref, kp_ref, vp_ref, t_ref, o_ref, l_ref = refs
    else:
        q_ref, kc_ref, vc_ref, t_ref, o_ref, l_ref = refs
    scale = HEAD_DIM ** -0.5
    for h in range(H_GROUP):
        sl = slice(HEAD_DIM * h, HEAD_DIM * (h + 1))
        qh = q_ref[0, :, sl].astype(BF16)
        s = _nt_dot(qh, kc_ref[0, :, sl].astype(BF16)) * scale + t_ref[0, h, :, A_BLK:]
        m = jnp.max(s, axis=1, keepdims=True)
        if use_prev:
            sp = _nt_dot(qh, kp_ref[0, :, sl].astype(BF16)) * scale + t_ref[0, h, :, :A_BLK]
            m = jnp.maximum(m, jnp.max(sp, axis=1, keepdims=True))
        e = jnp.exp(s - m)
        l = jnp.sum(e, axis=1, keepdims=True)
        o = _dot(e.astype(BF16), vc_ref[0, :, sl].astype(BF16))
        if use_prev:
            ep = jnp.exp(sp - m)
            l = l + jnp.sum(ep, axis=1, keepdims=True)
            o = o + _dot(ep.astype(BF16), vp_ref[0, :, sl].astype(BF16))
        o_ref[0, :, sl] = o / l
        l_ref[0, :, sl] = jnp.broadcast_to(m + jnp.log(l), (A_BLK, HEAD_DIM))


def _group_bias(bias_tab, g):
    window, dil = DILATED_CONFIGS[g]
    n_keys = window // dil + 1
    return bias_tab[g * H_GROUP:(g + 1) * H_GROUP, 0:n_keys * dil:dil]


def _dil_prompt_tables(bias_tab):
    tables = []
    for g in range(len(DILATED_CONFIGS)):
        bj = _group_bias(bias_tab, g)
        lo, hi = A_BLK - (2 * A_BLK - 1), A_BLK + A_BLK - 1
        v = jnp.full((H_GROUP, hi - lo + 1), NEG, F32)
        v = lax.dynamic_update_slice(v, bj, (0, -lo))
        t_gen = _toeplitz(v, A_BLK, 2 * A_BLK)
        t_first = t_gen.at[:, :, :A_BLK].set(NEG)
        tables.append(jnp.stack([t_first, t_gen]))
    return tables


def _dil_prompt_group(q, k, v, table, g, batch, seq):
    dil = DILATED_CONFIGS[g][1]
    u = seq // dil
    nqb = u // A_BLK
    use_prev = nqb > 1
    n_col = WIDTH // A_OUT
    qv = q.reshape(batch, u, dil * WIDTH)
    kv = k.reshape(batch, u, dil * WIDTH)
    vv = v.reshape(batch, u, dil * WIDTH)
    cur = lambda b, r, i: (b, i, r * n_col + g)
    prev = lambda b, r, i: (b, jnp.maximum(i - 1, 0), r * n_col + g)
    blk = (1, A_BLK, A_OUT)
    in_specs = [pl.BlockSpec(blk, cur), pl.BlockSpec(blk, cur), pl.BlockSpec(blk, cur)]
    args = [qv, kv, vv]
    if use_prev:
        in_specs += [pl.BlockSpec(blk, prev), pl.BlockSpec(blk, prev)]
        args += [kv, vv]
    in_specs += [pl.BlockSpec((1, H_GROUP, A_BLK, 2 * A_BLK), lambda b, r, i: (jnp.minimum(i, 1), 0, 0, 0))]
    args += [table]
    out_map = lambda b, r, i: (b, i, r)
    o, lse = pl.pallas_call(
        functools.partial(_dil_prompt_kernel, use_prev=use_prev),
        grid=(batch, dil, nqb),
        in_specs=in_specs,
        out_specs=[pl.BlockSpec(blk, out_map), pl.BlockSpec(blk, out_map)],
        out_shape=[jax.ShapeDtypeStruct((batch, u, dil * A_OUT), F32)] * 2,
        compiler_params=_cparams(("parallel", "parallel", "arbitrary")),
        name="dilated_prompt_g%d" % g,
    )(*args)
    return o.reshape(batch * seq, A_OUT), lse.reshape(batch * seq, A_OUT)


NEW_PAD = 16
A_ROWS = H_GROUP * DEC_SEQ


def _dil_sample_kernel(qbd_ref, kn_ref, vn_ref, k0, v0, k1, v1, k2, v2, tp0, tp1, tp2, tn_ref, o_ref):
    scale = HEAD_DIM ** -0.5
    outs, lses = [], []
    for g, (kr, vr, tp) in enumerate(((k0, v0, tp0), (k1, v1, tp1), (k2, v2, tp2))):
        gs = slice(A_OUT * g, A_OUT * (g + 1))
        qb = qbd_ref[0, g]
        s = _dot(qb, kr[0].astype(BF16)) * scale + tp[...]
        sn = _nt_dot(qb, kn_ref[0, :, gs].astype(BF16)) * scale + tn_ref[g]
        m = jnp.maximum(jnp.max(s, axis=1, keepdims=True), jnp.max(sn, axis=1, keepdims=True))
        e = jnp.exp(s - m)
        en = jnp.exp(sn - m)
        l = jnp.sum(e, axis=1, keepdims=True) + jnp.sum(en, axis=1, keepdims=True)
        o = _nt_dot(e.astype(BF16), vr[0].astype(BF16)) + _dot(en.astype(BF16), vn_ref[0, :, gs].astype(BF16))
        outs.append(o / l)
        lses.append(m + jnp.log(l))
    mm = jnp.maximum(jnp.maximum(lses[0], lses[1]), lses[2])
    es = [jnp.exp(x - mm) for x in lses]
    o_ref[0] = (es[0] * outs[0] + es[1] * outs[1] + es[2] * outs[2]) / (es[0] + es[1] + es[2])


def _dil_sample_tables(bias_tab):
    tps, tns = [], []
    tq = np.arange(DEC_SEQ)
    for g, (window, dil) in enumerate(DILATED_CONFIGS):
        bg = bias_tab[g * H_GROUP:(g + 1) * H_GROUP]
        length = window
        pos = np.arange(length)
        dist = length + tq[:, None] - pos[None, :]
        ok = (dist % dil == 0) & (dist <= window)
        rows = jnp.stack([bg[:, t + 1:t + 1 + length][:, ::-1] for t in range(DEC_SEQ)], axis=1)
        tps.append(jnp.where(jnp.asarray(ok)[None], rows, NEG).reshape(A_ROWS, length))
        tn_ = np.arange(NEW_PAD)
        dn = tq[:, None] - tn_[None, :]
        okn = (dn >= 0) & (tn_[None, :] < DEC_SEQ) & (dn % dil == 0)
        vals = bg[:, np.clip(dn, 0, DEC_SEQ)]
        tns.append(jnp.where(jnp.asarray(okn)[None], vals, NEG).reshape(A_ROWS, NEW_PAD))
    return tps, jnp.stack(tns)


def _pad_new(x, dec_batch, rows):
    x = x.reshape(dec_batch, DEC_SEQ, x.shape[-1])
    return jnp.pad(x, ((0, 0), (0, rows - DEC_SEQ), (0, 0)))


def _dil_sample(q, k, v, caches_k, caches_v, j, tps, tn, dec_batch):
    n_grp = len(DILATED_CONFIGS)
    sel = np.zeros((H_GROUP, DEC_SEQ, A_ROWS), np.float32)
    for h in range(H_GROUP):
        for t in range(DEC_SEQ):
            sel[h, t, h * DEC_SEQ + t] = 1.0
    q5 = q.reshape(dec_batch, DEC_SEQ, n_grp, H_GROUP, HEAD_DIM)
    qbd = jnp.einsum("btghd,htr->bgrhd", q5, jnp.asarray(sel), precision=lax.Precision.HIGHEST)
    qbd = qbd.reshape(dec_batch, n_grp, A_ROWS, A_OUT).astype(BF16)
    kn, vn = (_pad_new(a, dec_batch, NEW_PAD) for a in (k, v))
    args = [qbd, kn, vn]
    in_specs = [pl.BlockSpec((1, n_grp, A_ROWS, A_OUT), lambda b: (b, 0, 0, 0)),
                pl.BlockSpec((1, NEW_PAD, WIDTH), lambda b: (b, 0, 0)),
                pl.BlockSpec((1, NEW_PAD, WIDTH), lambda b: (b, 0, 0))]
    for g, (window, dil) in enumerate(DILATED_CONFIGS):
        for c in (caches_k[g], caches_v[g]):
            args.append(_positions_minor(c))
            in_specs.append(pl.BlockSpec((1, A_OUT, window), lambda b, j=j: (j * dec_batch + b, 0, 0)))
    for tp in tps:
        args.append(tp)
        in_specs.append(pl.BlockSpec(tp.shape, lambda b: (0, 0)))
    args.append(tn)
    in_specs.append(pl.BlockSpec(tn.shape, lambda b: (0, 0, 0)))
    o = pl.pallas_call(
        _dil_sample_kernel,
        grid=(dec_batch,),
        in_specs=in_specs,
        out_specs=pl.BlockSpec((1, A_ROWS, A_OUT), lambda b: (b, 0, 0)),
        out_shape=jax.ShapeDtypeStruct((dec_batch, A_ROWS, A_OUT), F32),
        compiler_params=_cparams(("parallel",)),
        name="dilated_sample",
    )(*args)
    o = o.reshape(dec_batch, H_GROUP, DEC_SEQ, H_GROUP, HEAD_DIM)
    idx = np.arange(H_GROUP)
    o = o[:, idx, :, idx, :]
    return jnp.transpose(o, (1, 2, 0, 3)).reshape(dec_batch * DEC_SEQ, A_OUT)


BC_BLK = 256
HEADS_PER_STEP = MXU_DIM // HEAD_DIM


def _lambda_value(lam_ref, lambda_init):
    lv = lam_ref[...]
    s1 = jnp.sum(lv[0:1] * lv[1:2], axis=1, keepdims=True)
    s2 = jnp.sum(lv[2:3] * lv[3:4], axis=1, keepdims=True)
    return jnp.exp(s1) - jnp.exp(s2) + lambda_init


LOG2E = 1.4426950408889634


def _diff_prompt_kernel(qt_ref, k_ref, vt_ref, ext_ref, lam_ref, sg_ref, o_ref, tab_scr, st_scr, p_scr, *,
                        lambda_init, nqb):
    b = pl.program_id(1)
    qi = pl.program_id(2)
    scale2 = DIFF_DIM ** -0.5 * LOG2E

    @pl.when((b == 0) & (qi == 0))
    def _():
        for hl in range(HEADS_PER_STEP):
            for dd in range(nqb):
                row = ext_ref[0, hl:hl + 1, dd * BC_BLK:(dd + 2) * BC_BLK] * LOG2E
                x = jnp.broadcast_to(row, (BC_BLK, 2 * BC_BLK))
                tab_scr[hl, dd] = pltpu.roll(x, 0, 1, stride=1, stride_axis=0)[:, BC_BLK:]

    streams = [(hl, half) for hl in range(HEADS_PER_STEP) for half in range(2)]
    qts = []
    for hl, half in streams:
        c0 = HEAD_DIM * hl + DIFF_DIM * half
        qts.append((qt_ref[0, c0:c0 + DIFF_DIM, :] * scale2).astype(BF16))

    def body(kb, carry):
        ms, ls, accs = carry
        keys = pl.ds(pl.multiple_of(kb * BC_BLK, BC_BLK), BC_BLK)
        ms_n, ls_n, accs_n, alphas = [], [], [], []
        for s, (hl, half) in enumerate(streams):
            c0 = HEAD_DIM * hl + DIFF_DIM * half
            ks = k_ref[0, keys, c0:c0 + DIFF_DIM].astype(BF16)
            st = _dot(ks, qts[s]) + tab_scr[hl, qi - kb]
            st_scr[s] = st
            ms_n.append(jnp.maximum(ms[s], jnp.max(st, axis=0, keepdims=True)))
        for s in range(len(streams)):
            alphas.append(jnp.exp2(ms[s] - ms_n[s]))
            p = jnp.exp2(st_scr[s] - ms_n[s])
            ls_n.append(alphas[s] * ls[s] + jnp.sum(p, axis=0, keepdims=True))
            p_scr[s] = p.astype(BF16)
        for s, (hl, half) in enumerate(streams):
            vt = vt_ref[0, HEAD_DIM * hl:HEAD_DIM * (hl + 1), keys].astype(BF16)
            accs_n.append(alphas[s] * accs[s] + _dot(vt, p_scr[s]))
        return tuple(ms_n), tuple(ls_n), tuple(accs_n)

    n_s = len(streams)
    init = (tuple(jnp.full((1, BC_BLK), NEG, F32) for _ in range(n_s)),
            tuple(jnp.zeros((1, BC_BLK), F32) for _ in range(n_s)),
            tuple(jnp.zeros((HEAD_DIM, BC_BLK), F32) for _ in range(n_s)))
    _, ls, accs = lax.fori_loop(0, qi + 1, body, init)

    lam = _lambda_value(lam_ref, lambda_init)
    outs = []
    for hl in range(HEADS_PER_STEP):
        o = accs[2 * hl] / ls[2 * hl] - lam * (accs[2 * hl + 1] / ls[2 * hl + 1])
        ms = jnp.mean(o * o, axis=0, keepdims=True)
        outs.append(o * lax.rsqrt(ms + EPS) * sg_ref[...] * (1.0 - lambda_init))
    o_ref[0] = jnp.concatenate(outs, axis=0).T


def _diff_prompt_bias_rows(bias_tab, seq):
    ext = jnp.concatenate([jnp.full((N_HEADS, BC_BLK), NEG, F32), bias_tab[:, :seq + BC_BLK]], axis=1)
    return ext.reshape(N_HEADS // HEADS_PER_STEP, HEADS_PER_STEP, seq + 2 * BC_BLK)


def _diff_prompt(qt, k, vt, ext, lam_vecs, sub_gain_col, lambda_init, batch, seq):
    nqb = seq // BC_BLK
    n_hg = WIDTH // MXU_DIM
    o = pl.pallas_call(
        functools.partial(_diff_prompt_kernel, lambda_init=lambda_init, nqb=nqb),
        grid=(n_hg, batch, nqb),
        in_specs=[pl.BlockSpec((1, MXU_DIM, BC_BLK), lambda h, b, i: (b, h, i)),
                  pl.BlockSpec((1, seq, MXU_DIM), lambda h, b, i: (b, 0, h)),
                  pl.BlockSpec((1, MXU_DIM, seq), lambda h, b, i: (b, h, 0)),
                  pl.BlockSpec((1, HEADS_PER_STEP, seq + 2 * BC_BLK), lambda h, b, i: (h, 0, 0)),
                  pl.BlockSpec((4, DIFF_DIM), lambda h, b, i: (0, 0)),
                  pl.BlockSpec((HEAD_DIM, 1), lambda h, b, i: (0, 0))],
        out_specs=pl.BlockSpec((1, BC_BLK, MXU_DIM), lambda h, b, i: (b, i, h)),
        out_shape=jax.ShapeDtypeStruct((batch, seq, WIDTH), F32),
        scratch_shapes=[pltpu.VMEM((HEADS_PER_STEP, nqb, BC_BLK, BC_BLK), F32),
                        pltpu.VMEM((2 * HEADS_PER_STEP, BC_BLK, BC_BLK), F32),
                        pltpu.VMEM((2 * HEADS_PER_STEP, BC_BLK, BC_BLK), BF16)],
        compiler_params=_cparams(("arbitrary", "arbitrary", "arbitrary")),
        name="diff_prompt",
    )(qt, k.reshape(batch, seq, WIDTH), vt, ext, lam_vecs, sub_gain_col)
    return o.reshape(batch * seq, WIDTH)


def _log_sigmoid(z):
    return jnp.minimum(z, 0.0) - jnp.log(1.0 + jnp.exp(-jnp.abs(z)))


def _suffix_sum_rows(x, tri):
    hi, lo = _split_bf16(x)
    return _dot(hi, tri) + _dot(lo, tri)


STICK_HEADS = MXU_DIM // HEAD_DIM
UNDERFLOW = 104.0


def _stick_prompt_kernel(qt_ref, k_ref, vt_ref, tri_ref, o_ref, lsc_scr, hilo_scr, a_scr):
    qi = pl.program_id(2)
    scale = HEAD_DIM ** -0.5
    tri = tri_ref[...]
    key_i = lax.broadcasted_iota(jnp.int32, (BC_BLK, BC_BLK), 0)
    qry_i = lax.broadcasted_iota(jnp.int32, (BC_BLK, BC_BLK), 1)
    strict = key_i < qry_i
    qts = [(qt_ref[0, HEAD_DIM * h:HEAD_DIM * (h + 1), :] * scale).astype(BF16) for h in range(STICK_HEADS)]

    def block(kb, cs, accs, masked):
        keys = pl.ds(pl.multiple_of(kb * BC_BLK, BC_BLK), BC_BLK)
        cs_n, accs_n, edge = [], [], []

        def stage_a(h):
            fs = slice(HEAD_DIM * h, HEAD_DIM * (h + 1))
            z = _dot(k_ref[0, keys, fs].astype(BF16), qts[h])
            neg_abs = lax.bitcast_convert_type(
                lax.bitcast_convert_type(z, jnp.uint32) | jnp.uint32(0x80000000), F32)
            t = jnp.log(1.0 + jnp.exp(neg_abs))
            ls = jnp.minimum(z, 0.0) - t
            stay = ls - z
            if masked:
                stay = jnp.where(strict, stay, 0.0)
            hi, lo = _split_bf16(stay)
            hilo_scr[h, :BC_BLK] = hi
            hilo_scr[h, BC_BLK:] = lo
            lsc_scr[h] = ls + cs[h]
            edge.append(stay[0:1] - ls[0:1])

        def stage_b(h):
            x = lsc_scr[h] + _dot(tri, hilo_scr[h])
            a = jnp.exp(x)
            if masked:
                a = jnp.where(strict, a, 0.0)
            a_scr[h] = a.astype(BF16)
            cs_n.append(x[0:1] + edge[h])

        def stage_c(h):
            fs = slice(HEAD_DIM * h, HEAD_DIM * (h + 1))
            accs_n.append(accs[h] + _dot(vt_ref[0, fs, keys].astype(BF16), a_scr[h]))

        for stage in (stage_a, stage_b, stage_c):
            for h in range(STICK_HEADS):
                stage(h)
        return tuple(cs_n), tuple(accs_n)

    def alive(cs):
        top = cs[0]
        for c in cs[1:]:
            top = jnp.maximum(top, c)
        return jnp.max(top) > -UNDERFLOW

    cs = tuple(jnp.zeros((1, BC_BLK), F32) for _ in range(STICK_HEADS))
    accs = tuple(jnp.zeros((HEAD_DIM, BC_BLK), F32) for _ in range(STICK_HEADS))
    cs, accs = block(qi, cs, accs, True)

    def cond(carry):
        kb, go, _, _ = carry
        return (kb >= 0) & go

    def body(carry):
        kb, _, cs, accs = carry
        cs, accs = block(kb, cs, accs, False)
        return kb - 1, alive(cs), cs, accs

    _, _, _, accs = lax.while_loop(cond, body, (qi - 1, alive(cs), cs, accs))
    o_ref[0] = jnp.concatenate(accs, axis=0).T


def _tri_later(n):
    r = np.arange(n)
    return jnp.asarray((r[:, None] > r[None, :]).astype(np.float32), dtype=BF16)


def _stick_prompt(qt, k, vt, batch, seq):
    nqb = seq // BC_BLK
    n_hg = WIDTH // MXU_DIM
    o = pl.pallas_call(
        _stick_prompt_kernel,
        grid=(n_hg, batch, nqb),
        in_specs=[pl.BlockSpec((1, MXU_DIM, BC_BLK), lambda h, b, i: (b, h, i)),
                  pl.BlockSpec((1, seq, MXU_DIM), lambda h, b, i: (b, 0, h)),
                  pl.BlockSpec((1, MXU_DIM, seq), lambda h, b, i: (b, h, 0)),
                  pl.BlockSpec((BC_BLK, 2 * BC_BLK), lambda h, b, i: (0, 0))],
        out_specs=pl.BlockSpec((1, BC_BLK, MXU_DIM), lambda h, b, i: (b, i, h)),
        out_shape=jax.ShapeDtypeStruct((batch, seq, WIDTH), F32),
        scratch_shapes=[pltpu.VMEM((STICK_HEADS, BC_BLK, BC_BLK), F32),
                        pltpu.VMEM((STICK_HEADS, 2 * BC_BLK, BC_BLK), BF16),
                        pltpu.VMEM((STICK_HEADS, BC_BLK, BC_BLK), BF16)],
        compiler_params=_cparams(("parallel", "parallel", "arbitrary")),
        name="stick_prompt",
    )(qt, k.reshape(batch, seq, WIDTH), vt, jnp.tile(_tri_later(BC_BLK).T, (1, 2)))
    return o.reshape(batch * seq, WIDTH)


HEADS_PER_GROUP = MXU_DIM // HEAD_DIM
N_VGROUPS = WIDTH // MXU_DIM
B_ROWS = 2 * HEADS_PER_GROUP * DEC_SEQ
C_ROWS = HEADS_PER_GROUP * DEC_SEQ
B_QROWS = N_VGROUPS * B_ROWS
C_QROWS = N_VGROUPS * C_ROWS
PAGES_PER_STEP = 4


def _page_scores(q_ref, k_page_ref):
    parts = []
    for c in range(N_VGROUPS):
        kt = k_page_ref[0, MXU_DIM * c:MXU_DIM * (c + 1), :].astype(BF16)
        parts.append(_dot(q_ref[0, c], kt))
    return jnp.concatenate(parts, axis=0)


def _page_values(w, v_page_ref, rows):
    outs = []
    for c in range(N_VGROUPS):
        vt = v_page_ref[0, MXU_DIM * c:MXU_DIM * (c + 1), :].astype(BF16)
        outs.append(_nt_dot(w[rows * c:rows * (c + 1), :].astype(BF16), vt))
    return outs


def _diff_sample_kernel(pt_ref, q_ref, *refs, lambda_init, n_steps):
    k_refs = refs[:PAGES_PER_STEP]
    v_refs = refs[PAGES_PER_STEP:2 * PAGES_PER_STEP]
    (kn_ref, vn_ref, bias_ref, biasn_ref, lam_ref, sg_ref, o_ref,
     s_scr, m_scr, l_scr, acc_scr) = refs[2 * PAGES_PER_STEP:]
    ph = pl.program_id(1)
    p = pl.program_id(2)
    scale = DIFF_DIM ** -0.5
    n_pages = n_steps * PAGES_PER_STEP

    def score_pages(pages):
        m = m_scr[...]
        for slot, k_ref, bias in pages:
            s = _page_scores(q_ref, k_ref) * scale + bias
            s_scr[slot] = s
            m = jnp.maximum(m, s)
        m_scr[...] = m

    def value_pages(pages):
        l = l_scr[...]
        accs = [acc_scr[c] for c in range(N_VGROUPS)]
        for slot, v_ref in pages:
            e = jnp.exp(s_scr[slot] - m_scr[...])
            l = l + e
            for c, d in enumerate(_page_values(e, v_ref, B_ROWS)):
                accs[c] = accs[c] + d
        l_scr[...] = l
        for c in range(N_VGROUPS):
            acc_scr[c] = accs[c]

    @pl.when((ph == 0) & (p == 0))
    def _():
        m_scr[...] = jnp.full(m_scr.shape, NEG, F32)
        l_scr[...] = jnp.zeros(l_scr.shape, F32)
        acc_scr[...] = jnp.zeros(acc_scr.shape, F32)

    @pl.when(ph == 0)
    def _():
        score_pages([(p * PAGES_PER_STEP + g, k_refs[g], bias_ref[:, PAGE_SIZE * g:PAGE_SIZE * (g + 1)])
                     for g in range(PAGES_PER_STEP)])

    @pl.when((ph == 0) & (p == n_steps - 1))
    def _():
        score_pages([(n_pages, kn_ref, biasn_ref[...])])

    @pl.when((ph == 1) & (p == 0))
    def _():
        m_scr[...] = jnp.broadcast_to(jnp.max(m_scr[...], axis=1, keepdims=True), m_scr.shape)

    @pl.when(ph == 1)
    def _():
        value_pages([(p * PAGES_PER_STEP + g, v_refs[g]) for g in range(PAGES_PER_STEP)])

    @pl.when((ph == 1) & (p == n_steps - 1))
    def _():
        value_pages([(n_pages, vn_ref)])
        lam = _lambda_value(lam_ref, lambda_init)
        l = jnp.sum(l_scr[...], axis=1, keepdims=True)
        half_rows = B_ROWS // 2
        r = lax.broadcasted_iota(jnp.int32, (half_rows, MXU_DIM), 0)
        ln = lax.broadcasted_iota(jnp.int32, (half_rows, MXU_DIM), 1)
        own = (r // DEC_SEQ) == (ln // HEAD_DIM)
        for c in range(N_VGROUPS):
            an = acc_scr[c] / l[B_ROWS * c:B_ROWS * (c + 1)]
            d = jnp.where(own, an[:half_rows] - lam * an[half_rows:], 0.0)
            ms = jnp.sum(d * d, axis=1, keepdims=True) * (1.0 / HEAD_DIM)
            o_ref[0, c] = d * lax.rsqrt(ms + EPS) * sg_ref[...] * (1.0 - lambda_init)


def _diag_blocks(x, dec_batch):
    x = x.reshape(dec_batch, N_VGROUPS, HEADS_PER_GROUP, DEC_SEQ, HEADS_PER_GROUP, HEAD_DIM)
    idx = np.arange(HEADS_PER_GROUP)
    x = x[:, :, idx, :, idx, :]
    x = jnp.transpose(x, (1, 3, 2, 0, 4))
    return x.reshape(dec_batch * DEC_SEQ, WIDTH)


def _query_tiles(q, dec_batch, sub_dim, rows, row_of):
    n_sub = MXU_DIM // sub_dim
    sel = np.zeros((n_sub, DEC_SEQ, rows), np.float32)
    for j in range(n_sub):
        for t in range(DEC_SEQ):
            sel[j, t, row_of(j, t)] = 1.0
    q5 = q.reshape(dec_batch, DEC_SEQ, N_VGROUPS, n_sub, sub_dim)
    qt = jnp.einsum("btcjd,jtr->bcrjd", q5, jnp.asarray(sel), precision=lax.Precision.HIGHEST)
    return qt.reshape(dec_batch, N_VGROUPS, rows, MXU_DIM).astype(BF16)


def _b_row(j, t):
    return (j % 2) * (B_ROWS // 2) + (j // 2) * DEC_SEQ + t


def _c_row(hl, t):
    return hl * DEC_SEQ + t


def _diff_sample_bias(bias_tab, past_len):
    segs = [bias_tab[:, t + 1:t + 1 + past_len][:, ::-1] for t in range(DEC_SEQ)]
    past = jnp.stack(segs, axis=1)
    tn_ = np.arange(PAGE_SIZE)[None, :]
    tq = np.arange(DEC_SEQ)[:, None]
    dist = tq - tn_
    ok = (dist >= 0) & (tn_ < DEC_SEQ)
    new = jnp.where(jnp.asarray(ok)[None], bias_tab[:, np.clip(dist, 0, DEC_SEQ)], NEG)
    full = jnp.concatenate([past, new], axis=2)
    n_cols = past_len + PAGE_SIZE
    full = full.reshape(N_VGROUPS, 1, HEADS_PER_GROUP, DEC_SEQ, n_cols)
    full = jnp.broadcast_to(full, (N_VGROUPS, 2, HEADS_PER_GROUP, DEC_SEQ, n_cols))
    return full.reshape(B_QROWS, n_cols)


def _new_page(x, dec_batch):
    x = jnp.transpose(x.reshape(dec_batch, DEC_SEQ, WIDTH), (0, 2, 1))
    return jnp.pad(x, ((0, 0), (0, 0), (0, PAGE_SIZE - DEC_SEQ)))


def _diff_sample(q, k, v, cache_k, cache_v, j, page_table, bias_sb, lam_vecs, sub_gain, lambda_init, dec_batch):
    n_pool = cache_k.shape[1]
    n_pages = page_table.shape[1]
    n_steps = n_pages // PAGES_PER_STEP
    ck = _positions_minor(cache_k)
    cv = _positions_minor(cache_v)
    qt = _query_tiles(q, dec_batch, DIFF_DIM, B_ROWS, _b_row)
    kn = _new_page(k, dec_batch)
    vn = _new_page(v, dec_batch)
    sg = jnp.tile(sub_gain, (1, HEADS_PER_GROUP))
    last = n_steps - 1
    page = (1, WIDTH, PAGE_SIZE)

    def k_map(g):
        return lambda b, ph, p, pt: (j * n_pool + pt[b, jnp.where(ph == 0, p, last) * PAGES_PER_STEP + g], 0, 0)

    def v_map(g):
        return lambda b, ph, p, pt: (j * n_pool + pt[b, jnp.where(ph == 1, p, 0) * PAGES_PER_STEP + g], 0, 0)

    per_b = lambda b, ph, p, pt: (b, 0, 0)
    in_specs = [pl.BlockSpec((1, N_VGROUPS, B_ROWS, MXU_DIM), lambda b, ph, p, pt: (b, 0, 0, 0))]
    in_specs += [pl.BlockSpec(page, k_map(g)) for g in range(PAGES_PER_STEP)]
    in_specs += [pl.BlockSpec(page, v_map(g)) for g in range(PAGES_PER_STEP)]
    in_specs += [pl.BlockSpec(page, per_b), pl.BlockSpec(page, per_b),
                 pl.BlockSpec((B_QROWS, PAGES_PER_STEP * PAGE_SIZE),
                              lambda b, ph, p, pt: (0, jnp.where(ph == 0, p, last))),
                 pl.BlockSpec((B_QROWS, PAGE_SIZE), lambda b, ph, p, pt: (0, n_pages)),
                 pl.BlockSpec((4, DIFF_DIM), lambda b, ph, p, pt: (0, 0)),
                 pl.BlockSpec((1, MXU_DIM), lambda b, ph, p, pt: (0, 0))]
    grid_spec = pltpu.PrefetchScalarGridSpec(
        num_scalar_prefetch=1,
        grid=(dec_batch, 2, n_steps),
        in_specs=in_specs,
        out_specs=pl.BlockSpec((1, N_VGROUPS, B_ROWS // 2, MXU_DIM), lambda b, ph, p, pt: (b, 0, 0, 0)),
        scratch_shapes=[pltpu.VMEM((n_pages + 1, B_QROWS, PAGE_SIZE), F32),
                        pltpu.VMEM((B_QROWS, PAGE_SIZE), F32),
                        pltpu.VMEM((B_QROWS, PAGE_SIZE), F32),
                        pltpu.VMEM((N_VGROUPS, B_ROWS, MXU_DIM), F32)])
    o = pl.pallas_call(
        functools.partial(_diff_sample_kernel, lambda_init=lambda_init, n_steps=n_steps),
        grid_spec=grid_spec,
        out_shape=jax.ShapeDtypeStruct((dec_batch, N_VGROUPS, B_ROWS // 2, MXU_DIM), F32),
        compiler_params=_cparams(("parallel", "arbitrary", "arbitrary")),
        name="diff_sample",
    )(page_table, qt, *([ck] * PAGES_PER_STEP), *([cv] * PAGES_PER_STEP), kn, vn, bias_sb, bias_sb, lam_vecs, sg)
    return _diag_blocks(o, dec_batch)


def _stick_sample_kernel(pt_ref, q_ref, *refs, n_steps):
    k_refs = refs[:PAGES_PER_STEP]
    v_refs = refs[PAGES_PER_STEP:2 * PAGES_PER_STEP]
    kn_ref, vn_ref, valid_ref, tri_ref, o_ref, c_scr, acc_scr = refs[2 * PAGES_PER_STEP:]
    p = pl.program_id(1)
    scale = HEAD_DIM ** -0.5

    def pages(items):
        c = c_scr[...]
        accs = [acc_scr[g] for g in range(N_VGROUPS)]
        parts = []
        for k_ref, v_ref, valid in items:
            z = _page_scores(q_ref, k_ref) * scale
            ls = _log_sigmoid(z)
            stay = ls - z
            if valid is not None:
                stay = stay * valid
            parts.append((ls, stay, jnp.sum(stay, axis=1, keepdims=True)))
        for (k_ref, v_ref, valid), (ls, stay, total) in zip(items, parts):
            after = _suffix_sum_rows(stay, tri_ref[...])
            a = jnp.exp(ls + after + c)
            if valid is not None:
                a = a * valid
            c = c + total
            for g, d in enumerate(_page_values(a, v_ref, C_ROWS)):
                accs[g] = accs[g] + d
        c_scr[...] = c
        for g in range(N_VGROUPS):
            acc_scr[g] = accs[g]

    @pl.when(p == 0)
    def _():
        c_scr[...] = jnp.zeros(c_scr.shape, F32)
        acc_scr[...] = jnp.zeros(acc_scr.shape, F32)
        pages([(kn_ref, vn_ref, valid_ref[...])])

    pages([(k_refs[g], v_refs[g], None) for g in range(PAGES_PER_STEP)])

    @pl.when(p == n_steps - 1)
    def _():
        o_ref[0] = acc_scr[...]


def _stick_sample(q, k, v, cache_k, cache_v, j, page_table, dec_batch):
    n_pool = cache_k.shape[1]
    n_pages = page_table.shape[1]
    n_steps = n_pages // PAGES_PER_STEP
    ck = _positions_minor(cache_k)
    cv = _positions_minor(cache_v)
    qt = _query_tiles(q, dec_batch, HEAD_DIM, C_ROWS, _c_row)
    kn = _new_page(k, dec_batch)
    vn = _new_page(v, dec_batch)
    valid = np.zeros((C_QROWS, PAGE_SIZE), np.float32)
    for c in range(N_VGROUPS):
        for hl in range(HEADS_PER_GROUP):
            for t in range(DEC_SEQ):
                valid[c * C_ROWS + _c_row(hl, t), :t] = 1.0
    page = (1, WIDTH, PAGE_SIZE)

    def page_map(g):
        return lambda b, p, pt: (j * n_pool + pt[b, n_pages - 1 - (p * PAGES_PER_STEP + g)], 0, 0)

    per_b = lambda b, p, pt: (b, 0, 0)
    in_specs = [pl.BlockSpec((1, N_VGROUPS, C_ROWS, MXU_DIM), lambda b, p, pt: (b, 0, 0, 0))]
    in_specs += [pl.BlockSpec(page, page_map(g)) for g in range(PAGES_PER_STEP)] * 2
    in_specs += [pl.BlockSpec(page, per_b), pl.BlockSpec(page, per_b),
                 pl.BlockSpec((C_QROWS, PAGE_SIZE), lambda b, p, pt: (0, 0)),
                 pl.BlockSpec((PAGE_SIZE, PAGE_SIZE), lambda b, p, pt: (0, 0))]
    grid_spec = pltpu.PrefetchScalarGridSpec(
        num_scalar_prefetch=1,
        grid=(dec_batch, n_steps),
        in_specs=in_specs,
        out_specs=pl.BlockSpec((1, N_VGROUPS, C_ROWS, MXU_DIM), lambda b, p, pt: (b, 0, 0, 0)),
        scratch_shapes=[pltpu.VMEM((C_QROWS, 1), F32),
                        pltpu.VMEM((N_VGROUPS, C_ROWS, MXU_DIM), F32)])
    o = pl.pallas_call(
        functools.partial(_stick_sample_kernel, n_steps=n_steps),
        grid_spec=grid_spec,
        out_shape=jax.ShapeDtypeStruct((dec_batch, N_VGROUPS, C_ROWS, MXU_DIM), F32),
        compiler_params=_cparams(("parallel", "arbitrary")),
        name="stick_sample",
    )(page_table, qt, *([ck] * PAGES_PER_STEP), *([cv] * PAGES_PER_STEP), kn, vn, jnp.asarray(valid),
      _tri_later(PAGE_SIZE))
    return _diag_blocks(o, dec_batch)


PROMPT_TM = 512


def _heads_last(xt, batch, seq, n_sub, sub_dim):
    return jnp.transpose(xt.reshape(batch, n_sub, sub_dim, seq), (0, 3, 1, 2))


def kernel(x_prompt, x_sample, cache_a_k_g0, cache_a_v_g0, cache_a_k_g1, cache_a_v_g1, cache_a_k_g2, cache_a_v_g2, cache_b_k, cache_b_v, cache_c_k, cache_c_v, page_table, norm_gain, rel_bias, w_in_a, w_out_a, q_gain_a, k_gain_a, w_in_b, w_out_b, q_gain_b, k_gain_b, lambda_q1, lambda_k1, lambda_q2, lambda_k2, sub_gain_b, w_in_c, w_out_c):
    batch, seq, _ = x_prompt.shape
    dec_batch = x_sample.shape[0]
    depth = norm_gain.shape[0]
    n_p = batch * seq
    n_s = dec_batch * DEC_SEQ
    caches_ak = (cache_a_k_g0, cache_a_k_g1, cache_a_k_g2)
    caches_av = (cache_a_v_g0, cache_a_v_g1, cache_a_v_g2)

    bias_tab = _bias_table(rel_bias)
    tabs_a = _dil_prompt_tables(bias_tab)
    tps_a, tn_a = _dil_sample_tables(bias_tab)
    ext_b = _diff_prompt_bias_rows(bias_tab, seq)
    bias_sb = _diff_sample_bias(bias_tab, page_table.shape[1] * PAGE_SIZE)

    y_p = x_prompt.reshape(n_p, D_MODEL)
    y_s = x_sample.reshape(n_s, D_MODEL)
    a_kp = [[] for _ in DILATED_CONFIGS]
    a_vp = [[] for _ in DILATED_CONFIGS]
    a_ks, a_vs = [], []
    b_kp, b_vp, b_ks, b_vs = [], [], [], []
    c_kp, c_vp, c_ks, c_vs = [], [], [], []
    no_t = ("n", "n", "n", "n")

    for i in range(depth):
        kind, j = i % 3, i // 3
        g = norm_gain[i][None, :]
        if kind == 0:
            w = w_in_a[j].astype(BF16)
            wq, wk, wv, wg = (w[:, :WIDTH], w[:, WIDTH:2 * WIDTH], w[:, 2 * WIDTH:3 * WIDTH], w[:, 3 * WIDTH:])
            gq = jnp.tile(q_gain_a[j], N_HEADS)[None, :]
            gk = jnp.tile(k_gain_a[j], N_HEADS)[None, :]
            w_out = w_out_a[j].astype(BF16)
            norms = (HEAD_DIM, HEAD_DIM, 0)
            q, k, kt, v, vt = _proj_in(y_p, g, [wq, wk, wv], [gq, gk], norms, ("n", "nt", "nt"), PROMPT_TM, seq)
            gate, = _proj_in(y_p, g, [wg], [], (0,), no_t[:1], PROMPT_TM, seq)
            os_, ls_ = [], []
            for gi in range(len(DILATED_CONFIGS)):
                o_g, l_g = _dil_prompt_group(q, k, v, tabs_a[gi], gi, batch, seq)
                os_.append(o_g)
                ls_.append(l_g)
            y_p = _proj_out(os_ + ls_, gate, y_p, w_out, PROMPT_TM)
            for gi, (window, _) in enumerate(DILATED_CONFIGS):
                keep = min(window, seq)
                fs = slice(gi * A_OUT, (gi + 1) * A_OUT)
                a_kp[gi].append(_heads_last(kt[:, fs, seq - keep:], batch, keep, H_GROUP, HEAD_DIM))
                a_vp[gi].append(_heads_last(vt[:, fs, seq - keep:], batch, keep, H_GROUP, HEAD_DIM))
            q, k, v = _proj_in(y_s, g, [wq, wk, wv], [gq, gk], norms, no_t[:3], n_s, n_s)
            gate, = _proj_in(y_s, g, [wg], [], (0,), no_t[:1], n_s, n_s)
            o = _dil_sample(q, k, v, caches_ak, caches_av, j, tps_a, tn_a, dec_batch)
            y_s = _proj_out([o], gate, y_s, w_out, n_s)
            a_ks.append(k.reshape(dec_batch, DEC_SEQ, N_HEADS, HEAD_DIM))
            a_vs.append(v.reshape(dec_batch, DEC_SEQ, N_HEADS, HEAD_DIM))
        else:
            w_in = (w_in_b if kind == 1 else w_in_c)[j].astype(BF16)
            w_out = (w_out_b if kind == 1 else w_out_c)[j].astype(BF16)
            ws = [w_in[:, c * WIDTH:(c + 1) * WIDTH] for c in range(4)]
            ws_p = [ws[0].T, ws[1], ws[2].T, ws[3]]
            or_p = ("t", "nt", "t", "n")
            if kind == 1:
                lambda_init = float(0.8 - 0.6 * math.exp(-0.3 * i))
                gq = jnp.tile(q_gain_b[j], 2 * N_HEADS)[None, :]
                gk = jnp.tile(k_gain_b[j], 2 * N_HEADS)[None, :]
                groups = (DIFF_DIM, DIFF_DIM, 0, 0)
                gains_p, gains_s = [gq.T, gk], [gq, gk]
                lam_vecs = jnp.stack([lambda_q1[j], lambda_k1[j], lambda_q2[j], lambda_k2[j]])
                sub_gain = sub_gain_b[j][None, :]
            else:
                groups = (0, 0, 0, 0)
                gains_p, gains_s = [], []
            qt, k, kt, vt, gate = _proj_in(y_p, g, ws_p, gains_p, groups, or_p, PROMPT_TM, seq)
            qs, ks_, vs_, gate_s = _proj_in(y_s, g, ws, gains_s, groups, no_t, n_s, n_s)
            if kind == 1:
                o = _diff_prompt(qt, k, vt, ext_b, lam_vecs, sub_gain.T, lambda_init, batch, seq)
                o_s = _diff_sample(qs, ks_, vs_, cache_b_k, cache_b_v, j, page_table, bias_sb, lam_vecs,
                                   sub_gain, lambda_init, dec_batch)
                b_kp.append(_heads_last(kt, batch, seq, 2 * N_HEADS, DIFF_DIM))
                b_vp.append(_heads_last(vt, batch, seq, N_HEADS, HEAD_DIM))
                b_ks.append(ks_.reshape(dec_batch, DEC_SEQ, 2 * N_HEADS, DIFF_DIM))
                b_vs.append(vs_.reshape(dec_batch, DEC_SEQ, N_HEADS, HEAD_DIM))
            else:
                o = _stick_prompt(qt, k, vt, batch, seq)
                o_s = _stick_sample(qs, ks_, vs_, cache_c_k, cache_c_v, j, page_table, dec_batch)
                c_kp.append(_heads_last(kt, batch, seq, N_HEADS, HEAD_DIM))
                c_vp.append(_heads_last(vt, batch, seq, N_HEADS, HEAD_DIM))
                c_ks.append(ks_.reshape(dec_batch, DEC_SEQ, N_HEADS, HEAD_DIM))
                c_vs.append(vs_.reshape(dec_batch, DEC_SEQ, N_HEADS, HEAD_DIM))
            y_p = _proj_out([o], gate, y_p, w_out, PROMPT_TM)
            y_s = _proj_out([o_s], gate_s, y_s, w_out, n_s)

    return (y_p.reshape(batch, seq, D_MODEL), y_s.reshape(dec_batch, DEC_SEQ, D_MODEL),
            jnp.stack(a_kp[0]), jnp.stack(a_vp[0]), jnp.stack(a_kp[1]), jnp.stack(a_vp[1]),
            jnp.stack(a_kp[2]), jnp.stack(a_vp[2]), jnp.stack(a_ks), jnp.stack(a_vs),
            jnp.stack(b_kp), jnp.stack(b_vp), jnp.stack(b_ks), jnp.stack(b_vs),
            jnp.stack(c_kp), jnp.stack(c_vp), jnp.stack(c_ks), jnp.stack(c_vs))
```

```python
import functools
import math

import numpy as np
import jax
import jax.numpy as jnp
from jax import lax
from jax.experimental import pallas as pl
from jax.experimental.pallas import tpu as pltpu

F32 = jnp.float32
BF16 = jnp.bfloat16

D_MODEL = 1024
N_HEADS = 24
HEAD_DIM = 64
WIDTH = N_HEADS * HEAD_DIM
H_GROUP = 8
A_OUT = H_GROUP * HEAD_DIM
DIFF_DIM = HEAD_DIM // 2
DILATED_CONFIGS = ((128, 1), (512, 4), (2048, 16))
NUM_BUCKETS = 32
REL_MAX_DIST = 2048
EPS = 1e-6
PAGE_SIZE = 128
DEC_SEQ = 4
NEG = -1e30

LANES = 128
MXU_DIM = 256
VMEM_LIMIT = 52 * 1024 * 1024

BIAS_TABLE_LEN = 8320


def _cparams(sem):
    return pltpu.CompilerParams(dimension_semantics=sem, vmem_limit_bytes=VMEM_LIMIT)


def _nt_dot(a, b):
    return lax.dot_general(a, b, (((1,), (1,)), ((), ())), preferred_element_type=F32)


def _dot(a, b):
    return jnp.dot(a, b, preferred_element_type=F32)


def _split_bf16(x):
    hi = x.astype(BF16)
    lo = (x - hi.astype(F32)).astype(BF16)
    return hi, lo


def _positions_minor(cache):
    n, pool, pos, heads, dim = cache.shape
    return jnp.transpose(cache, (0, 1, 3, 4, 2)).reshape(n * pool, heads * dim, pos)


def _bucket_thresholds():
    max_exact = NUM_BUCKETS // 2
    n = np.arange(1, 1 << 14, dtype=np.float64)
    large = max_exact + (np.log(n / max_exact) / math.log(REL_MAX_DIST / max_exact)
                         * (NUM_BUCKETS - max_exact)).astype(np.int64)
    return tuple(int(n[large >= max_exact + k][0]) for k in range(1, NUM_BUCKETS - max_exact))


_BUCKET_THRESHOLDS = _bucket_thresholds()


def _bias_table_kernel(bt_ref, o_ref):
    max_exact = NUM_BUCKETS // 2
    n = lax.broadcasted_iota(jnp.int32, (N_HEADS, BIAS_TABLE_LEN), 1)
    large = jnp.full(n.shape, max_exact, jnp.int32)
    for thr in _BUCKET_THRESHOLDS:
        large = large + (n >= thr).astype(jnp.int32)
    bucket = jnp.where(n < max_exact, n, jnp.minimum(large, NUM_BUCKETS - 1))
    bt = bt_ref[...]
    acc = jnp.zeros(n.shape, F32)
    for b in range(NUM_BUCKETS):
        acc = jnp.where(bucket == b, jnp.broadcast_to(bt[:, b:b + 1], n.shape), acc)
    o_ref[...] = acc


def _bias_table(rel_bias):
    return pl.pallas_call(
        _bias_table_kernel,
        out_shape=jax.ShapeDtypeStruct((N_HEADS, BIAS_TABLE_LEN), F32),
        name="bias_table",
    )(rel_bias.T)


def _toeplitz(v, n_rows, n_cols):
    h = v.shape[0]
    p = n_rows + n_cols + 1
    pad = jnp.zeros((h, 1), v.dtype)
    w = jnp.concatenate([pad, v[:, ::-1], pad], axis=1)
    x = jnp.tile(w[:, None, :], (1, n_rows, 1)).reshape(h, n_rows * p)
    x = x[:, :n_rows * (p - 1)].reshape(h, n_rows, p - 1)
    return x[:, :, n_rows:n_rows + n_cols]


PROJ_TN = 512


def _group_sumsq(acc, bd, axis):
    hi, lo = _split_bf16(acc * acc)
    parts = []
    for c in range(PROJ_TN // MXU_DIM):
        sl = slice(c * MXU_DIM, (c + 1) * MXU_DIM)
        if axis == 1:
            parts.append(_dot(hi[:, sl], bd) + _dot(lo[:, sl], bd))
        else:
            parts.append(_dot(bd, hi[sl, :]) + _dot(bd, lo[sl, :]))
    return jnp.concatenate(parts, axis=axis)


def _proj_in_kernel(*refs, groups, orient):
    n_w = len(groups)
    n_g = sum(1 for gs in groups if gs)
    x_ref, g_ref = refs[0], refs[1]
    w_refs = refs[2:2 + n_w]
    gain_refs = list(refs[2 + n_w:2 + n_w + n_g])
    bd_ref = refs[2 + n_w + n_g]
    out_refs = list(refs[3 + n_w + n_g:-1])
    h_scr = refs[-1]

    @pl.when(pl.program_id(1) == 0)
    def _():
        x = x_ref[...]
        ms = jnp.mean(x * x, axis=-1, keepdims=True)
        h_scr[...] = (x * lax.rsqrt(ms + EPS) * g_ref[...]).astype(BF16)

    h = h_scr[...]
    for k in range(n_w):
        gs = groups[k]
        if orient[k] == "t":
            acc = _nt_dot(w_refs[k][...], h)
        else:
            acc = _dot(h, w_refs[k][...])
        if gs:
            bd = bd_ref[0 if gs == HEAD_DIM else 1]
            ss = _group_sumsq(acc, bd, 0 if orient[k] == "t" else 1)
            acc = acc * lax.rsqrt(ss * (1.0 / gs) + EPS) * gain_refs.pop(0)[...]
        if orient[k] == "t":
            out_refs.pop(0)[0] = acc
        else:
            out_refs.pop(0)[...] = acc
            if orient[k] == "nt":
                out_refs.pop(0)[0] = acc.T


def _block_diag_ones():
    r = np.arange(MXU_DIM)
    b64 = (r[:, None] // HEAD_DIM == r[None, :] // HEAD_DIM)
    b32 = (r[:, None] // DIFF_DIM == r[None, :] // DIFF_DIM)
    return jnp.asarray(np.stack([b64, b32]).astype(np.float32), dtype=BF16)


def _proj_in(x, g, weights, gains, groups, orient, tm, seq):
    m, d = x.shape
    tn = PROJ_TN
    n_w = len(weights)
    w_dim = weights[0].shape[0] if orient[0] == "t" else weights[0].shape[1]
    spb = seq // tm
    in_specs = [pl.BlockSpec((tm, d), lambda i, j: (i, 0)),
                pl.BlockSpec((1, d), lambda i, j: (0, 0))]
    for k in range(n_w):
        if orient[k] == "t":
            in_specs.append(pl.BlockSpec((tn, d), lambda i, j: (j, 0)))
        else:
            in_specs.append(pl.BlockSpec((d, tn), lambda i, j: (0, j)))
    for k in range(n_w):
        if groups[k]:
            if orient[k] == "t":
                in_specs.append(pl.BlockSpec((tn, 1), lambda i, j: (j, 0)))
            else:
                in_specs.append(pl.BlockSpec((1, tn), lambda i, j: (0, j)))
    in_specs.append(pl.BlockSpec((2, MXU_DIM, MXU_DIM), lambda i, j: (0, 0, 0)))
    out_specs, out_shape = [], []
    for k in range(n_w):
        if "n" in orient[k]:
            out_specs.append(pl.BlockSpec((tm, tn), lambda i, j: (i, j)))
            out_shape.append(jax.ShapeDtypeStruct((m, w_dim), F32))
        if "t" in orient[k]:
            out_specs.append(pl.BlockSpec((1, tn, tm), lambda i, j: (i // spb, j, i % spb)))
            out_shape.append(jax.ShapeDtypeStruct((m // seq, w_dim, seq), F32))
    return pl.pallas_call(
        functools.partial(_proj_in_kernel, groups=tuple(groups), orient=tuple(orient)),
        grid=(m // tm, w_dim // tn),
        in_specs=in_specs,
        out_specs=out_specs,
        out_shape=out_shape,
        scratch_shapes=[pltpu.VMEM((tm, d), BF16)],
        compiler_params=_cparams(("parallel", "arbitrary")),
        name="proj_in",
    )(x, g, *weights, *gains, _block_diag_ones())


def _proj_out_kernel(*refs, merge):
    if merge:
        o0, o1, o2, l0, l1, l2, gate_ref, x_ref, w_ref, out_ref = refs
        la, lb, lc = l0[...], l1[...], l2[...]
        m = jnp.maximum(jnp.maximum(la, lb), lc)
        ea, eb, ec = jnp.exp(la - m), jnp.exp(lb - m), jnp.exp(lc - m)
        o = (ea * o0[...] + eb * o1[...] + ec * o2[...]) / (ea + eb + ec)
    else:
        o_ref, gate_ref, x_ref, w_ref, out_ref = refs
        o = o_ref[...]
    gate = gate_ref[...]
    act = o * (gate / (1.0 + jnp.exp(-gate)))
    out_ref[...] = x_ref[...] + _dot(act.astype(BF16), w_ref[...])


def _proj_out(o_list, gate, x, w, tm):
    m, wd = gate.shape
    d = x.shape[1]
    merge = len(o_list) > 1
    row = lambda i: (i, 0)
    in_specs = [pl.BlockSpec((tm, wd), row) for _ in o_list]
    in_specs += [pl.BlockSpec((tm, wd), row), pl.BlockSpec((tm, d), row),
                 pl.BlockSpec((wd, d), lambda i: (0, 0))]
    return pl.pallas_call(
        functools.partial(_proj_out_kernel, merge=merge),
        grid=(m // tm,),
        in_specs=in_specs,
        out_specs=pl.BlockSpec((tm, d), row),
        out_shape=jax.ShapeDtypeStruct((m, d), F32),
        compiler_params=_cparams(("parallel",)),
        name="proj_out",
    )(*o_list, gate, x, w)


A_BLK = 128


A_HEADS = LANES // HEAD_DIM
A_UNROLL = 4


def _tn_dot(a, b):
    return lax.dot_general(a, b, (((0,), (0,)), ((), ())), preferred_element_type=F32)


def _dil_prompt_kernel(*refs, dil, use_prev):
    if use_prev:
        q_ref, kc_ref, vc_ref, kp_ref, vp_ref, t_ref, o_ref, l_ref = refs
    else:
        q_ref, kc_ref, vc_ref, t_ref, o_ref, l_ref = refs
    scale = HEAD_DIM ** -0.5
    first = jnp.minimum(pl.program_id(2), 1)

    def residues(chains):
        heads = [(i, h) for i in range(len(chains)) for h in range(A_HEADS)]
        rows = [c[0] for c in chains]
        tab = [c[2] for c in chains]
        q2 = [(q_ref[0, rw, :] * scale).astype(BF16) for rw in rows]
        kc2 = [kc_ref[0, rw, :].astype(BF16) for rw in rows]
        sl = [slice(HEAD_DIM * h, HEAD_DIM * (h + 1)) for h in range(A_HEADS)]
        st = [_nt_dot(kc2[i][:, sl[h]], q2[i][:, sl[h]]) + t_ref[tab[i], h, A_BLK:, :] for i, h in heads]
        m = [jnp.max(s, axis=0, keepdims=True) for s in st]
        if use_prev:
            kp2 = [(kp_ref if c[1][0] else kc_ref)[0, c[1][1], :].astype(BF16) for c in chains]
            sp = [_nt_dot(kp2[i][:, sl[h]], q2[i][:, sl[h]]) + t_ref[tab[i], h, :A_BLK, :] for i, h in heads]
            m = [jnp.maximum(a, jnp.max(s, axis=0, keepdims=True)) for a, s in zip(m, sp)]
        e = [jnp.exp(s - a) for s, a in zip(st, m)]
        l = [jnp.sum(x, axis=0, keepdims=True) for x in e]
        vc2 = [vc_ref[0, rw, :].astype(BF16) for rw in rows]
        o = [_tn_dot(vc2[i][:, sl[h]], e[n].astype(BF16)) for n, (i, h) in enumerate(heads)]
        if use_prev:
            ep = [jnp.exp(s - a) for s, a in zip(sp, m)]
            l = [a + jnp.sum(x, axis=0, keepdims=True) for a, x in zip(l, ep)]
            vp2 = [(vp_ref if c[1][0] else vc_ref)[0, c[1][1], :].astype(BF16) for c in chains]
            o = [a + _tn_dot(vp2[i][:, sl[h]], ep[n].astype(BF16)) for n, ((i, h), a) in enumerate(zip(heads, o))]
        o = [a / b for a, b in zip(o, l)]
        lse = [jnp.broadcast_to(a + jnp.log(b), (HEAD_DIM, A_BLK)) for a, b in zip(m, l)]
        for i, rw in enumerate(rows):
            o_ref[0, rw, :] = jnp.concatenate(o[A_HEADS * i:A_HEADS * (i + 1)], axis=0).T
            l_ref[0, rw, :] = jnp.concatenate(lse[A_HEADS * i:A_HEADS * (i + 1)], axis=0).T

    def strided(r):
        rw = pl.ds(r, A_BLK, stride=dil)
        return (rw, (True, rw), first)

    if dil == 1:
        chains = [(pl.ds(0, A_BLK), (True, pl.ds(0, A_BLK)), first)]
        chains += [(pl.ds(A_BLK * j, A_BLK), (False, pl.ds(A_BLK * (j - 1), A_BLK)), 1) for j in range(1, A_UNROLL)]
        residues(chains)
    elif dil <= A_UNROLL:
        residues([strided(r) for r in range(dil)])
    else:
        def body(i, carry):
            residues([strided(i * A_UNROLL + u) for u in range(A_UNROLL)])
            return carry
        lax.fori_loop(0, dil // A_UNROLL, body, 0)


def _group_bias(bias_tab, g):
    window, dil = DILATED_CONFIGS[g]
    n_keys = window // dil + 1
    return bias_tab[g * H_GROUP:(g + 1) * H_GROUP, 0:n_keys * dil:dil]


def _dil_prompt_tables(bias_tab):
    tables = []
    for g in range(len(DILATED_CONFIGS)):
        bj = _group_bias(bias_tab, g)
        lo, hi = A_BLK - (2 * A_BLK - 1), A_BLK + A_BLK - 1
        v = jnp.full((H_GROUP, hi - lo + 1), NEG, F32)
        v = lax.dynamic_update_slice(v, bj, (0, -lo))
        t_gen = _toeplitz(v, A_BLK, 2 * A_BLK)
        t_first = t_gen.at[:, :, :A_BLK].set(NEG)
        tables.append(jnp.swapaxes(jnp.stack([t_first, t_gen]), 2, 3))
    return tables


def _dil_prompt_group(q, k, v, table, g, batch, seq):
    dil = DILATED_CONFIGS[g][1]
    rows = A_BLK * (dil if dil > 1 else A_UNROLL)
    prev_rows = rows if dil > 1 else A_BLK
    nqb = seq // rows
    use_prev = nqb > 1
    n_hp = A_OUT // LANES
    q3, k3, v3 = (a.reshape(batch, seq, WIDTH) for a in (q, k, v))
    cur = lambda b, h, i: (b, i, g * n_hp + h)
    prev = lambda b, h, i: (b, jnp.maximum(i * (rows // prev_rows) - 1, 0), g * n_hp + h)
    blk = (1, rows, LANES)
    in_specs = [pl.BlockSpec(blk, cur), pl.BlockSpec(blk, cur), pl.BlockSpec(blk, cur)]
    args = [q3, k3, v3]
    if use_prev:
        in_specs += [pl.BlockSpec((1, prev_rows, LANES), prev)] * 2
        args += [k3, v3]
    in_specs += [pl.BlockSpec((2, A_HEADS, 2 * A_BLK, A_BLK), lambda b, h, i: (0, h, 0, 0))]
    args += [table]
    out_map = lambda b, h, i: (b, i, h)
    o, lse = pl.pallas_call(
        functools.partial(_dil_prompt_kernel, dil=dil, use_prev=use_prev),
        grid=(batch, n_hp, nqb),
        in_specs=in_specs,
        out_specs=[pl.BlockSpec(blk, out_map), pl.BlockSpec(blk, out_map)],
        out_shape=[jax.ShapeDtypeStruct((batch, seq, A_OUT), F32)] * 2,
        compiler_params=_cparams(("parallel", "parallel", "arbitrary")),
        name="dilated_prompt_g%d" % g,
    )(*args)
    return o.reshape(batch * seq, A_OUT), lse.reshape(batch * seq, A_OUT)


NEW_PAD = 16
A_ROWS = H_GROUP * DEC_SEQ


def _dil_sample_kernel(qbd_ref, kn_ref, vn_ref, k0, v0, k1, v1, k2, v2, tp0, tp1, tp2, tn_ref, o_ref):
    scale = HEAD_DIM ** -0.5
    outs, lses = [], []
    for g, (kr, vr, tp) in enumerate(((k0, v0, tp0), (k1, v1, tp1), (k2, v2, tp2))):
        gs = slice(A_OUT * g, A_OUT * (g + 1))
        qb = qbd_ref[0, g]
        s = _dot(qb, kr[0].astype(BF16)) * scale + tp[...]
        sn = _nt_dot(qb, kn_ref[0, :, gs].astype(BF16)) * scale + tn_ref[g]
        m = jnp.maximum(jnp.max(s, axis=1, keepdims=True), jnp.max(sn, axis=1, keepdims=True))
        e = jnp.exp(s - m)
        en = jnp.exp(sn - m)
        l = jnp.sum(e, axis=1, keepdims=True) + jnp.sum(en, axis=1, keepdims=True)
        o = _nt_dot(e.astype(BF16), vr[0].astype(BF16)) + _dot(en.astype(BF16), vn_ref[0, :, gs].astype(BF16))
        outs.append(o / l)
        lses.append(m + jnp.log(l))
    mm = jnp.maximum(jnp.maximum(lses[0], lses[1]), lses[2])
    es = [jnp.exp(x - mm) for x in lses]
    o_ref[0] = (es[0] * outs[0] + es[1] * outs[1] + es[2] * outs[2]) / (es[0] + es[1] + es[2])


def _dil_sample_tables(bias_tab):
    tps, tns = [], []
    tq = np.arange(DEC_SEQ)
    for g, (window, dil) in enumerate(DILATED_CONFIGS):
        bg = bias_tab[g * H_GROUP:(g + 1) * H_GROUP]
        length = window
        pos = np.arange(length)
        dist = length + tq[:, None] - pos[None, :]
        ok = (dist % dil == 0) & (dist <= window)
        rows = jnp.stack([bg[:, t + 1:t + 1 + length][:, ::-1] for t in range(DEC_SEQ)], axis=1)
        tps.append(jnp.where(jnp.asarray(ok)[None], rows, NEG).reshape(A_ROWS, length))
        tn_ = np.arange(NEW_PAD)
        dn = tq[:, None] - tn_[None, :]
        okn = (dn >= 0) & (tn_[None, :] < DEC_SEQ) & (dn % dil == 0)
        vals = bg[:, np.clip(dn, 0, DEC_SEQ)]
        tns.append(jnp.where(jnp.asarray(okn)[None], vals, NEG).reshape(A_ROWS, NEW_PAD))
    return tps, jnp.stack(tns)


def _pad_new(x, dec_batch, rows):
    x = x.reshape(dec_batch, DEC_SEQ, x.shape[-1])
    return jnp.pad(x, ((0, 0), (0, rows - DEC_SEQ), (0, 0)))


def _dil_sample(q, k, v, caches_k, caches_v, j, tps, tn, dec_batch):
    n_grp = len(DILATED_CONFIGS)
    sel = np.zeros((H_GROUP, DEC_SEQ, A_ROWS), np.float32)
    for h in range(H_GROUP):
        for t in range(DEC_SEQ):
            sel[h, t, h * DEC_SEQ + t] = 1.0
    q5 = q.reshape(dec_batch, DEC_SEQ, n_grp, H_GROUP, HEAD_DIM)
    qbd = jnp.einsum("btghd,htr->bgrhd", q5, jnp.asarray(sel), precision=lax.Precision.HIGHEST)
    qbd = qbd.reshape(dec_batch, n_grp, A_ROWS, A_OUT).astype(BF16)
    kn, vn = (_pad_new(a, dec_batch, NEW_PAD) for a in (k, v))
    args = [qbd, kn, vn]
    in_specs = [pl.BlockSpec((1, n_grp, A_ROWS, A_OUT), lambda b: (b, 0, 0, 0)),
                pl.BlockSpec((1, NEW_PAD, WIDTH), lambda b: (b, 0, 0)),
                pl.BlockSpec((1, NEW_PAD, WIDTH), lambda b: (b, 0, 0))]
    for g, (window, dil) in enumerate(DILATED_CONFIGS):
        for c in (caches_k[g], caches_v[g]):
            args.append(_positions_minor(c))
            in_specs.append(pl.BlockSpec((1, A_OUT, window), lambda b, j=j: (j * dec_batch + b, 0, 0)))
    for tp in tps:
        args.append(tp)
        in_specs.append(pl.BlockSpec(tp.shape, lambda b: (0, 0)))
    args.append(tn)
    in_specs.append(pl.BlockSpec(tn.shape, lambda b: (0, 0, 0)))
    o = pl.pallas_call(
        _dil_sample_kernel,
        grid=(dec_batch,),
        in_specs=in_specs,
        out_specs=pl.BlockSpec((1, A_ROWS, A_OUT), lambda b: (b, 0, 0)),
        out_shape=jax.ShapeDtypeStruct((dec_batch, A_ROWS, A_OUT), F32),
        compiler_params=_cparams(("parallel",)),
        name="dilated_sample",
    )(*args)
    o = o.reshape(dec_batch, H_GROUP, DEC_SEQ, H_GROUP, HEAD_DIM)
    idx = np.arange(H_GROUP)
    o = o[:, idx, :, idx, :]
    return jnp.transpose(o, (1, 2, 0, 3)).reshape(dec_batch * DEC_SEQ, A_OUT)


BC_BLK = 256
HEADS_PER_STEP = MXU_DIM // HEAD_DIM


def _lambda_value(lam_ref, lambda_init):
    lv = lam_ref[...]
    s1 = jnp.sum(lv[0:1] * lv[1:2], axis=1, keepdims=True)
    s2 = jnp.sum(lv[2:3] * lv[3:4], axis=1, keepdims=True)
    return jnp.exp(s1) - jnp.exp(s2) + lambda_init


LOG2E = 1.4426950408889634


def _diff_prompt_kernel(qt_ref, k_ref, vt_ref, ext_ref, lam_ref, sg_ref, o_ref, tab_scr, st_scr, p_scr, *,
                        lambda_init, nqb):
    b = pl.program_id(1)
    qi = pl.program_id(2)
    scale2 = DIFF_DIM ** -0.5 * LOG2E

    @pl.when((b == 0) & (qi == 0))
    def _():
        for hl in range(HEADS_PER_STEP):
            for dd in range(nqb):
                row = ext_ref[0, hl:hl + 1, dd * BC_BLK:(dd + 2) * BC_BLK] * LOG2E
                x = jnp.broadcast_to(row, (BC_BLK, 2 * BC_BLK))
                tab_scr[hl, dd] = pltpu.roll(x, 0, 1, stride=1, stride_axis=0)[:, BC_BLK:]

    streams = [(hl, half) for hl in range(HEADS_PER_STEP) for half in range(2)]
    qts = []
    for hl, half in streams:
        c0 = HEAD_DIM * hl + DIFF_DIM * half
        qts.append((qt_ref[0, c0:c0 + DIFF_DIM, :] * scale2).astype(BF16))

    def body(kb, carry):
        ms, ls, accs = carry
        keys = pl.ds(pl.multiple_of(kb * BC_BLK, BC_BLK), BC_BLK)
        ms_n, ls_n, accs_n, alphas = [], [], [], []
        for s, (hl, half) in enumerate(streams):
            c0 = HEAD_DIM * hl + DIFF_DIM * half
            ks = k_ref[0, keys, c0:c0 + DIFF_DIM].astype(BF16)
            st = _dot(ks, qts[s]) + tab_scr[hl, qi - kb]
            st_scr[s] = st
            ms_n.append(jnp.maximum(ms[s], jnp.max(st, axis=0, keepdims=True)))
        for s in range(len(streams)):
            alphas.append(jnp.exp2(ms[s] - ms_n[s]))
            p = jnp.exp2(st_scr[s] - ms_n[s])
            ls_n.append(alphas[s] * ls[s] + jnp.sum(p, axis=0, keepdims=True))
            p_scr[s] = p.astype(BF16)
        for s, (hl, half) in enumerate(streams):
            vt = vt_ref[0, HEAD_DIM * hl:HEAD_DIM * (hl + 1), keys].astype(BF16)
            accs_n.append(alphas[s] * accs[s] + _dot(vt, p_scr[s]))
        return tuple(ms_n), tuple(ls_n), tuple(accs_n)

    n_s = len(streams)
    init = (tuple(jnp.full((1, BC_BLK), NEG, F32) for _ in range(n_s)),
            tuple(jnp.zeros((1, BC_BLK), F32) for _ in range(n_s)),
            tuple(jnp.zeros((HEAD_DIM, BC_BLK), F32) for _ in range(n_s)))
    _, ls, accs = lax.fori_loop(0, qi + 1, body, init)

    lam = _lambda_value(lam_ref, lambda_init)
    outs = []
    for hl in range(HEADS_PER_STEP):
        o = accs[2 * hl] / ls[2 * hl] - lam * (accs[2 * hl + 1] / ls[2 * hl + 1])
        ms = jnp.mean(o * o, axis=0, keepdims=True)
        outs.append(o * lax.rsqrt(ms + EPS) * sg_ref[...] * (1.0 - lambda_init))
    o_ref[0] = jnp.concatenate(outs, axis=0).T


def _diff_prompt_bias_rows(bias_tab, seq):
    ext = jnp.concatenate([jnp.full((N_HEADS, BC_BLK), NEG, F32), bias_tab[:, :seq + BC_BLK]], axis=1)
    return ext.reshape(N_HEADS // HEADS_PER_STEP, HEADS_PER_STEP, seq + 2 * BC_BLK)


def _diff_prompt(qt, k, vt, ext, lam_vecs, sub_gain_col, lambda_init, batch, seq):
    nqb = seq // BC_BLK
    n_hg = WIDTH // MXU_DIM
    o = pl.pallas_call(
        functools.partial(_diff_prompt_kernel, lambda_init=lambda_init, nqb=nqb),
        grid=(n_hg, batch, nqb),
        in_specs=[pl.BlockSpec((1, MXU_DIM, BC_BLK), lambda h, b, i: (b, h, i)),
                  pl.BlockSpec((1, seq, MXU_DIM), lambda h, b, i: (b, 0, h)),
                  pl.BlockSpec((1, MXU_DIM, seq), lambda h, b, i: (b, h, 0)),
                  pl.BlockSpec((1, HEADS_PER_STEP, seq + 2 * BC_BLK), lambda h, b, i: (h, 0, 0)),
                  pl.BlockSpec((4, DIFF_DIM), lambda h, b, i: (0, 0)),
                  pl.BlockSpec((HEAD_DIM, 1), lambda h, b, i: (0, 0))],
        out_specs=pl.BlockSpec((1, BC_BLK, MXU_DIM), lambda h, b, i: (b, i, h)),
        out_shape=jax.ShapeDtypeStruct((batch, seq, WIDTH), F32),
        scratch_shapes=[pltpu.VMEM((HEADS_PER_STEP, nqb, BC_BLK, BC_BLK), F32),
                        pltpu.VMEM((2 * HEADS_PER_STEP, BC_BLK, BC_BLK), F32),
                        pltpu.VMEM((2 * HEADS_PER_STEP, BC_BLK, BC_BLK), BF16)],
        compiler_params=_cparams(("arbitrary", "arbitrary", "arbitrary")),
        name="diff_prompt",
    )(qt, k.reshape(batch, seq, WIDTH), vt, ext, lam_vecs, sub_gain_col)
    return o.reshape(batch * seq, WIDTH)


def _log_sigmoid(z):
    return jnp.minimum(z, 0.0) - jnp.log(1.0 + jnp.exp(-jnp.abs(z)))


def _suffix_sum_rows(x, tri):
    hi, lo = _split_bf16(x)
    return _dot(hi, tri) + _dot(lo, tri)


STICK_HEADS = MXU_DIM // HEAD_DIM
UNDERFLOW = 104.0


def _stick_prompt_kernel(qt_ref, k_ref, vt_ref, tri_ref, o_ref, lsc_scr, hilo_scr, a_scr):
    qi = pl.program_id(2)
    scale = HEAD_DIM ** -0.5
    tri = tri_ref[...]
    key_i = lax.broadcasted_iota(jnp.int32, (BC_BLK, BC_BLK), 0)
    qry_i = lax.broadcasted_iota(jnp.int32, (BC_BLK, BC_BLK), 1)
    strict = key_i < qry_i
    qts = [(qt_ref[0, HEAD_DIM * h:HEAD_DIM * (h + 1), :] * scale).astype(BF16) for h in range(STICK_HEADS)]

    def block(kb, cs, accs, masked):
        keys = pl.ds(pl.multiple_of(kb * BC_BLK, BC_BLK), BC_BLK)
        cs_n, accs_n, edge = [], [], []

        def stage_a(h):
            fs = slice(HEAD_DIM * h, HEAD_DIM * (h + 1))
            z = _dot(k_ref[0, keys, fs].astype(BF16), qts[h])
            neg_abs = lax.bitcast_convert_type(
                lax.bitcast_convert_type(z, jnp.uint32) | jnp.uint32(0x80000000), F32)
            t = jnp.log(1.0 + jnp.exp(neg_abs))
            ls = jnp.minimum(z, 0.0) - t
            stay = ls - z
            if masked:
                stay = jnp.where(strict, stay, 0.0)
            hi, lo = _split_bf16(stay)
            hilo_scr[h, :BC_BLK] = hi
            hilo_scr[h, BC_BLK:] = lo
            lsc_scr[h] = ls + cs[h]
            edge.append(stay[0:1] - ls[0:1])

        def stage_b(h):
            x = lsc_scr[h] + _dot(tri, hilo_scr[h])
            a = jnp.exp(x)
            if masked:
                a = jnp.where(strict, a, 0.0)
            a_scr[h] = a.astype(BF16)
            cs_n.append(x[0:1] + edge[h])

        def stage_c(h):
            fs = slice(HEAD_DIM * h, HEAD_DIM * (h + 1))
            accs_n.append(accs[h] + _dot(vt_ref[0, fs, keys].astype(BF16), a_scr[h]))

        for stage in (stage_a, stage_b, stage_c):
            for h in range(STICK_HEADS):
                stage(h)
        return tuple(cs_n), tuple(accs_n)

    def alive(cs):
        top = cs[0]
        for c in cs[1:]:
            top = jnp.maximum(top, c)
        return jnp.max(top) > -UNDERFLOW

    cs = tuple(jnp.zeros((1, BC_BLK), F32) for _ in range(STICK_HEADS))
    accs = tuple(jnp.zeros((HEAD_DIM, BC_BLK), F32) for _ in range(STICK_HEADS))
    cs, accs = block(qi, cs, accs, True)

    def cond(carry):
        kb, go, _, _ = carry
        return (kb >= 0) & go

    def body(carry):
        kb, _, cs, accs = carry
        cs, accs = block(kb, cs, accs, False)
        return kb - 1, alive(cs), cs, accs

    _, _, _, accs = lax.while_loop(cond, body, (qi - 1, alive(cs), cs, accs))
    o_ref[0] = jnp.concatenate(accs, axis=0).T


def _tri_later(n):
    r = np.arange(n)
    return jnp.asarray((r[:, None] > r[None, :]).astype(np.float32), dtype=BF16)


def _stick_prompt(qt, k, vt, batch, seq):
    nqb = seq // BC_BLK
    n_hg = WIDTH // MXU_DIM
    o = pl.pallas_call(
        _stick_prompt_kernel,
        grid=(n_hg, batch, nqb),
        in_specs=[pl.BlockSpec((1, MXU_DIM, BC_BLK), lambda h, b, i: (b, h, i)),
                  pl.BlockSpec((1, seq, MXU_DIM), lambda h, b, i: (b, 0, h)),
                  pl.BlockSpec((1, MXU_DIM, seq), lambda h, b, i: (b, h, 0)),
                  pl.BlockSpec((BC_BLK, 2 * BC_BLK), lambda h, b, i: (0, 0))],
        out_specs=pl.BlockSpec((1, BC_BLK, MXU_DIM), lambda h, b, i: (b, i, h)),
        out_shape=jax.ShapeDtypeStruct((batch, seq, WIDTH), F32),
        scratch_shapes=[pltpu.VMEM((STICK_HEADS, BC_BLK, BC_BLK), F32),
                        pltpu.VMEM((STICK_HEADS, 2 * BC_BLK, BC_BLK), BF16),
                        pltpu.VMEM((STICK_HEADS, BC_BLK, BC_BLK), BF16)],
        compiler_params=_cparams(("parallel", "parallel", "arbitrary")),
        name="stick_prompt",
    )(qt, k.reshape(batch, seq, WIDTH), vt, jnp.tile(_tri_later(BC_BLK).T, (1, 2)))
    return o.reshape(batch * seq, WIDTH)


HEADS_PER_GROUP = MXU_DIM // HEAD_DIM
N_VGROUPS = WIDTH // MXU_DIM
B_ROWS = 2 * HEADS_PER_GROUP * DEC_SEQ
C_ROWS = HEADS_PER_GROUP * DEC_SEQ
B_QROWS = N_VGROUPS * B_ROWS
C_QROWS = N_VGROUPS * C_ROWS
PAGES_PER_STEP = 4


def _page_scores(q_ref, k_page_ref):
    parts = []
    for c in range(N_VGROUPS):
        kt = k_page_ref[0, MXU_DIM * c:MXU_DIM * (c + 1), :].astype(BF16)
        parts.append(_dot(q_ref[0, c], kt))
    return jnp.concatenate(parts, axis=0)


def _page_values(w, v_page_ref, rows):
    outs = []
    for c in range(N_VGROUPS):
        vt = v_page_ref[0, MXU_DIM * c:MXU_DIM * (c + 1), :].astype(BF16)
        outs.append(_nt_dot(w[rows * c:rows * (c + 1), :].astype(BF16), vt))
    return outs


def _diff_sample_kernel(pt_ref, q_ref, *refs, lambda_init, n_steps):
    k_refs = refs[:PAGES_PER_STEP]
    v_refs = refs[PAGES_PER_STEP:2 * PAGES_PER_STEP]
    (kn_ref, vn_ref, bias_ref, biasn_ref, lam_ref, sg_ref, o_ref,
     s_scr, m_scr, l_scr, acc_scr) = refs[2 * PAGES_PER_STEP:]
    ph = pl.program_id(1)
    p = pl.program_id(2)
    scale = DIFF_DIM ** -0.5
    n_pages = n_steps * PAGES_PER_STEP

    def score_pages(pages):
        m = m_scr[...]
        for slot, k_ref, bias in pages:
            s = _page_scores(q_ref, k_ref) * scale + bias
            s_scr[slot] = s
            m = jnp.maximum(m, s)
        m_scr[...] = m

    def value_pages(pages):
        l = l_scr[...]
        accs = [acc_scr[c] for c in range(N_VGROUPS)]
        for slot, v_ref in pages:
            e = jnp.exp(s_scr[slot] - m_scr[...])
            l = l + e
            for c, d in enumerate(_page_values(e, v_ref, B_ROWS)):
                accs[c] = accs[c] + d
        l_scr[...] = l
        for c in range(N_VGROUPS):
            acc_scr[c] = accs[c]

    @pl.when((ph == 0) & (p == 0))
    def _():
        m_scr[...] = jnp.full(m_scr.shape, NEG, F32)
        l_scr[...] = jnp.zeros(l_scr.shape, F32)
        acc_scr[...] = jnp.zeros(acc_scr.shape, F32)

    @pl.when(ph == 0)
    def _():
        score_pages([(p * PAGES_PER_STEP + g, k_refs[g], bias_ref[:, PAGE_SIZE * g:PAGE_SIZE * (g + 1)])
                     for g in range(PAGES_PER_STEP)])

    @pl.when((ph == 0) & (p == n_steps - 1))
    def _():
        score_pages([(n_pages, kn_ref, biasn_ref[...])])

    @pl.when((ph == 1) & (p == 0))
    def _():
        m_scr[...] = jnp.broadcast_to(jnp.max(m_scr[...], axis=1, keepdims=True), m_scr.shape)

    @pl.when(ph == 1)
    def _():
        value_pages([(p * PAGES_PER_STEP + g, v_refs[g]) for g in range(PAGES_PER_STEP)])

    @pl.when((ph == 1) & (p == n_steps - 1))
    def _():
        value_pages([(n_pages, vn_ref)])
        lam = _lambda_value(lam_ref, lambda_init)
        l = jnp.sum(l_scr[...], axis=1, keepdims=True)
        half_rows = B_ROWS // 2
        r = lax.broadcasted_iota(jnp.int32, (half_rows, MXU_DIM), 0)
        ln = lax.broadcasted_iota(jnp.int32, (half_rows, MXU_DIM), 1)
        own = (r // DEC_SEQ) == (ln // HEAD_DIM)
        for c in range(N_VGROUPS):
            an = acc_scr[c] / l[B_ROWS * c:B_ROWS * (c + 1)]
            d = jnp.where(own, an[:half_rows] - lam * an[half_rows:], 0.0)
            ms = jnp.sum(d * d, axis=1, keepdims=True) * (1.0 / HEAD_DIM)
            o_ref[0, c] = d * lax.rsqrt(ms + EPS) * sg_ref[...] * (1.0 - lambda_init)


def _diag_blocks(x, dec_batch):
    x = x.reshape(dec_batch, N_VGROUPS, HEADS_PER_GROUP, DEC_SEQ, HEADS_PER_GROUP, HEAD_DIM)
    idx = np.arange(HEADS_PER_GROUP)
    x = x[:, :, idx, :, idx, :]
    x = jnp.transpose(x, (1, 3, 2, 0, 4))
    return x.reshape(dec_batch * DEC_SEQ, WIDTH)


def _query_tiles(q, dec_batch, sub_dim, rows, row_of):
    n_sub = MXU_DIM // sub_dim
    sel = np.zeros((n_sub, DEC_SEQ, rows), np.float32)
    for j in range(n_sub):
        for t in range(DEC_SEQ):
            sel[j, t, row_of(j, t)] = 1.0
    q5 = q.reshape(dec_batch, DEC_SEQ, N_VGROUPS, n_sub, sub_dim)
    qt = jnp.einsum("btcjd,jtr->bcrjd", q5, jnp.asarray(sel), precision=lax.Precision.HIGHEST)
    return qt.reshape(dec_batch, N_VGROUPS, rows, MXU_DIM).astype(BF16)


def _b_row(j, t):
    return (j % 2) * (B_ROWS // 2) + (j // 2) * DEC_SEQ + t


def _c_row(hl, t):
    return hl * DEC_SEQ + t


def _diff_sample_bias(bias_tab, past_len):
    segs = [bias_tab[:, t + 1:t + 1 + past_len][:, ::-1] for t in range(DEC_SEQ)]
    past = jnp.stack(segs, axis=1)
    tn_ = np.arange(PAGE_SIZE)[None, :]
    tq = np.arange(DEC_SEQ)[:, None]
    dist = tq - tn_
    ok = (dist >= 0) & (tn_ < DEC_SEQ)
    new = jnp.where(jnp.asarray(ok)[None], bias_tab[:, np.clip(dist, 0, DEC_SEQ)], NEG)
    full = jnp.concatenate([past, new], axis=2)
    n_cols = past_len + PAGE_SIZE
    full = full.reshape(N_VGROUPS, 1, HEADS_PER_GROUP, DEC_SEQ, n_cols)
    full = jnp.broadcast_to(full, (N_VGROUPS, 2, HEADS_PER_GROUP, DEC_SEQ, n_cols))
    return full.reshape(B_QROWS, n_cols)


def _new_page(x, dec_batch):
    x = jnp.transpose(x.reshape(dec_batch, DEC_SEQ, WIDTH), (0, 2, 1))
    return jnp.pad(x, ((0, 0), (0, 0), (0, PAGE_SIZE - DEC_SEQ)))


def _diff_sample(q, k, v, cache_k, cache_v, j, page_table, bias_sb, lam_vecs, sub_gain, lambda_init, dec_batch):
    n_pool = cache_k.shape[1]
    n_pages = page_table.shape[1]
    n_steps = n_pages // PAGES_PER_STEP
    ck = _positions_minor(cache_k)
    cv = _positions_minor(cache_v)
    qt = _query_tiles(q, dec_batch, DIFF_DIM, B_ROWS, _b_row)
    kn = _new_page(k, dec_batch)
    vn = _new_page(v, dec_batch)
    sg = jnp.tile(sub_gain, (1, HEADS_PER_GROUP))
    last = n_steps - 1
    page = (1, WIDTH, PAGE_SIZE)

    def k_map(g):
        return lambda b, ph, p, pt: (j * n_pool + pt[b, jnp.where(ph == 0, p, last) * PAGES_PER_STEP + g], 0, 0)

    def v_map(g):
        return lambda b, ph, p, pt: (j * n_pool + pt[b, jnp.where(ph == 1, p, 0) * PAGES_PER_STEP + g], 0, 0)

    per_b = lambda b, ph, p, pt: (b, 0, 0)
    in_specs = [pl.BlockSpec((1, N_VGROUPS, B_ROWS, MXU_DIM), lambda b, ph, p, pt: (b, 0, 0, 0))]
    in_specs += [pl.BlockSpec(page, k_map(g)) for g in range(PAGES_PER_STEP)]
    in_specs += [pl.BlockSpec(page, v_map(g)) for g in range(PAGES_PER_STEP)]
    in_specs += [pl.BlockSpec(page, per_b), pl.BlockSpec(page, per_b),
                 pl.BlockSpec((B_QROWS, PAGES_PER_STEP * PAGE_SIZE),
                              lambda b, ph, p, pt: (0, jnp.where(ph == 0, p, last))),
                 pl.BlockSpec((B_QROWS, PAGE_SIZE), lambda b, ph, p, pt: (0, n_pages)),
                 pl.BlockSpec((4, DIFF_DIM), lambda b, ph, p, pt: (0, 0)),
                 pl.BlockSpec((1, MXU_DIM), lambda b, ph, p, pt: (0, 0))]
    grid_spec = pltpu.PrefetchScalarGridSpec(
        num_scalar_prefetch=1,
        grid=(dec_batch, 2, n_steps),
        in_specs=in_specs,
        out_specs=pl.BlockSpec((1, N_VGROUPS, B_ROWS // 2, MXU_DIM), lambda b, ph, p, pt: (b, 0, 0, 0)),
        scratch_shapes=[pltpu.VMEM((n_pages + 1, B_QROWS, PAGE_SIZE), F32),
                        pltpu.VMEM((B_QROWS, PAGE_SIZE), F32),
                        pltpu.VMEM((B_QROWS, PAGE_SIZE), F32),
                        pltpu.VMEM((N_VGROUPS, B_ROWS, MXU_DIM), F32)])
    o = pl.pallas_call(
        functools.partial(_diff_sample_kernel, lambda_init=lambda_init, n_steps=n_steps),
        grid_spec=grid_spec,
        out_shape=jax.ShapeDtypeStruct((dec_batch, N_VGROUPS, B_ROWS // 2, MXU_DIM), F32),
        compiler_params=_cparams(("parallel", "arbitrary", "arbitrary")),
        name="diff_sample",
    )(page_table, qt, *([ck] * PAGES_PER_STEP), *([cv] * PAGES_PER_STEP), kn, vn, bias_sb, bias_sb, lam_vecs, sg)
    return _diag_blocks(o, dec_batch)


def _stick_sample_kernel(pt_ref, q_ref, *refs, n_steps):
    k_refs = refs[:PAGES_PER_STEP]
    v_refs = refs[PAGES_PER_STEP:2 * PAGES_PER_STEP]
    kn_ref, vn_ref, valid_ref, tri_ref, o_ref, c_scr, acc_scr = refs[2 * PAGES_PER_STEP:]
    p = pl.program_id(1)
    scale = HEAD_DIM ** -0.5

    def pages(items):
        c = c_scr[...]
        accs = [acc_scr[g] for g in range(N_VGROUPS)]
        parts = []
        for k_ref, v_ref, valid in items:
            z = _page_scores(q_ref, k_ref) * scale
            ls = _log_sigmoid(z)
            stay = ls - z
            if valid is not None:
                stay = stay * valid
            parts.append((ls, stay, jnp.sum(stay, axis=1, keepdims=True)))
        for (k_ref, v_ref, valid), (ls, stay, total) in zip(items, parts):
            after = _suffix_sum_rows(stay, tri_ref[...])
            a = jnp.exp(ls + after + c)
            if valid is not None:
                a = a * valid
            c = c + total
            for g, d in enumerate(_page_values(a, v_ref, C_ROWS)):
                accs[g] = accs[g] + d
        c_scr[...] = c
        for g in range(N_VGROUPS):
            acc_scr[g] = accs[g]

    @pl.when(p == 0)
    def _():
        c_scr[...] = jnp.zeros(c_scr.shape, F32)
        acc_scr[...] = jnp.zeros(acc_scr.shape, F32)
        pages([(kn_ref, vn_ref, valid_ref[...])])

    pages([(k_refs[g], v_refs[g], None) for g in range(PAGES_PER_STEP)])

    @pl.when(p == n_steps - 1)
    def _():
        o_ref[0] = acc_scr[...]


def _stick_sample(q, k, v, cache_k, cache_v, j, page_table, dec_batch):
    n_pool = cache_k.shape[1]
    n_pages = page_table.shape[1]
    n_steps = n_pages // PAGES_PER_STEP
    ck = _positions_minor(cache_k)
    cv = _positions_minor(cache_v)
    qt = _query_tiles(q, dec_batch, HEAD_DIM, C_ROWS, _c_row)
    kn = _new_page(k, dec_batch)
    vn = _new_page(v, dec_batch)
    valid = np.zeros((C_QROWS, PAGE_SIZE), np.float32)
    for c in range(N_VGROUPS):
        for hl in range(HEADS_PER_GROUP):
            for t in range(DEC_SEQ):
                valid[c * C_ROWS + _c_row(hl, t), :t] = 1.0
    page = (1, WIDTH, PAGE_SIZE)

    def page_map(g):
        return lambda b, p, pt: (j * n_pool + pt[b, n_pages - 1 - (p * PAGES_PER_STEP + g)], 0, 0)

    per_b = lambda b, p, pt: (b, 0, 0)
    in_specs = [pl.BlockSpec((1, N_VGROUPS, C_ROWS, MXU_DIM), lambda b, p, pt: (b, 0, 0, 0))]
    in_specs += [pl.BlockSpec(page, page_map(g)) for g in range(PAGES_PER_STEP)] * 2
    in_specs += [pl.BlockSpec(page, per_b), pl.BlockSpec(page, per_b),
                 pl.BlockSpec((C_QROWS, PAGE_SIZE), lambda b, p, pt: (0, 0)),
                 pl.BlockSpec((PAGE_SIZE, PAGE_SIZE), lambda b, p, pt: (0, 0))]
    grid_spec = pltpu.PrefetchScalarGridSpec(
        num_scalar_prefetch=1,
        grid=(dec_batch, n_steps),
        in_specs=in_specs,
        out_specs=pl.BlockSpec((1, N_VGROUPS, C_ROWS, MXU_DIM), lambda b, p, pt: (b, 0, 0, 0)),
        scratch_shapes=[pltpu.VMEM((C_QROWS, 1), F32),
                        pltpu.VMEM((N_VGROUPS, C_ROWS, MXU_DIM), F32)])
    o = pl.pallas_call(
        functools.partial(_stick_sample_kernel, n_steps=n_steps),
        grid_spec=grid_spec,
        out_shape=jax.ShapeDtypeStruct((dec_batch, N_VGROUPS, C_ROWS, MXU_DIM), F32),
        compiler_params=_cparams(("parallel", "arbitrary")),
        name="stick_sample",
    )(page_table, qt, *([ck] * PAGES_PER_STEP), *([cv] * PAGES_PER_STEP), kn, vn, jnp.asarray(valid),
      _tri_later(PAGE_SIZE))
    return _diag_blocks(o, dec_batch)


PROMPT_TM = 1024


def _heads_last(xt, batch, seq, n_sub, sub_dim):
    return jnp.transpose(xt.reshape(batch, n_sub, sub_dim, seq), (0, 3, 1, 2))


def kernel(x_prompt, x_sample, cache_a_k_g0, cache_a_v_g0, cache_a_k_g1, cache_a_v_g1, cache_a_k_g2, cache_a_v_g2, cache_b_k, cache_b_v, cache_c_k, cache_c_v, page_table, norm_gain, rel_bias, w_in_a, w_out_a, q_gain_a, k_gain_a, w_in_b, w_out_b, q_gain_b, k_gain_b, lambda_q1, lambda_k1, lambda_q2, lambda_k2, sub_gain_b, w_in_c, w_out_c):
    batch, seq, _ = x_prompt.shape
    dec_batch = x_sample.shape[0]
    depth = norm_gain.shape[0]
    n_p = batch * seq
    n_s = dec_batch * DEC_SEQ
    caches_ak = (cache_a_k_g0, cache_a_k_g1, cache_a_k_g2)
    caches_av = (cache_a_v_g0, cache_a_v_g1, cache_a_v_g2)

    bias_tab = _bias_table(rel_bias)
    tabs_a = _dil_prompt_tables(bias_tab)
    tps_a, tn_a = _dil_sample_tables(bias_tab)
    ext_b = _diff_prompt_bias_rows(bias_tab, seq)
    bias_sb = _diff_sample_bias(bias_tab, page_table.shape[1] * PAGE_SIZE)

    y_p = x_prompt.reshape(n_p, D_MODEL)
    y_s = x_sample.reshape(n_s, D_MODEL)
    a_kp = [[] for _ in DILATED_CONFIGS]
    a_vp = [[] for _ in DILATED_CONFIGS]
    a_ks, a_vs = [], []
    b_kp, b_vp, b_ks, b_vs = [], [], [], []
    c_kp, c_vp, c_ks, c_vs = [], [], [], []
    no_t = ("n", "n", "n", "n")

    for i in range(depth):
        kind, j = i % 3, i // 3
        g = norm_gain[i][None, :]
        if kind == 0:
            w = w_in_a[j].astype(BF16)
            wq, wk, wv, wg = (w[:, :WIDTH], w[:, WIDTH:2 * WIDTH], w[:, 2 * WIDTH:3 * WIDTH], w[:, 3 * WIDTH:])
            gq = jnp.tile(q_gain_a[j], N_HEADS)[None, :]
            gk = jnp.tile(k_gain_a[j], N_HEADS)[None, :]
            w_out = w_out_a[j].astype(BF16)
            norms = (HEAD_DIM, HEAD_DIM, 0)
            q, k, kt, v, vt = _proj_in(y_p, g, [wq, wk, wv], [gq, gk], norms, ("n", "nt", "nt"), PROMPT_TM, seq)
            gate, = _proj_in(y_p, g, [wg], [], (0,), no_t[:1], PROMPT_TM, seq)
            os_, ls_ = [], []
            for gi in range(len(DILATED_CONFIGS)):
                o_g, l_g = _dil_prompt_group(q, k, v, tabs_a[gi], gi, batch, seq)
                os_.append(o_g)
                ls_.append(l_g)
            y_p = _proj_out(os_ + ls_, gate, y_p, w_out, PROMPT_TM)
            for gi, (window, _) in enumerate(DILATED_CONFIGS):
                keep = min(window, seq)
                fs = slice(gi * A_OUT, (gi + 1) * A_OUT)
                a_kp[gi].append(_heads_last(kt[:, fs, seq - keep:], batch, keep, H_GROUP, HEAD_DIM))
                a_vp[gi].append(_heads_last(vt[:, fs, seq - keep:], batch, keep, H_GROUP, HEAD_DIM))
            q, k, v = _proj_in(y_s, g, [wq, wk, wv], [gq, gk], norms, no_t[:3], n_s, n_s)
            gate, = _proj_in(y_s, g, [wg], [], (0,), no_t[:1], n_s, n_s)
            o = _dil_sample(q, k, v, caches_ak, caches_av, j, tps_a, tn_a, dec_batch)
            y_s = _proj_out([o], gate, y_s, w_out, n_s)
            a_ks.append(k.reshape(dec_batch, DEC_SEQ, N_HEADS, HEAD_DIM))
            a_vs.append(v.reshape(dec_batch, DEC_SEQ, N_HEADS, HEAD_DIM))
        else:
            w_in = (w_in_b if kind == 1 else w_in_c)[j].astype(BF16)
            w_out = (w_out_b if kind == 1 else w_out_c)[j].astype(BF16)
            ws = [w_in[:, c * WIDTH:(c + 1) * WIDTH] for c in range(4)]
            ws_p = [ws[0].T, ws[1], ws[2].T, ws[3]]
            or_p = ("t", "nt", "t", "n")
            if kind == 1:
                lambda_init = float(0.8 - 0.6 * math.exp(-0.3 * i))
                gq = jnp.tile(q_gain_b[j], 2 * N_HEADS)[None, :]
                gk = jnp.tile(k_gain_b[j], 2 * N_HEADS)[None, :]
                groups = (DIFF_DIM, DIFF_DIM, 0, 0)
                gains_p, gains_s = [gq.T, gk], [gq, gk]
                lam_vecs = jnp.stack([lambda_q1[j], lambda_k1[j], lambda_q2[j], lambda_k2[j]])
                sub_gain = sub_gain_b[j][None, :]
            else:
                groups = (0, 0, 0, 0)
                gains_p, gains_s = [], []
            qt, k, kt, vt, gate = _proj_in(y_p, g, ws_p, gains_p, groups, or_p, PROMPT_TM, seq)
            qs, ks_, vs_, gate_s = _proj_in(y_s, g, ws, gains_s, groups, no_t, n_s, n_s)
            if kind == 1:
                o = _diff_prompt(qt, k, vt, ext_b, lam_vecs, sub_gain.T, lambda_init, batch, seq)
                o_s = _diff_sample(qs, ks_, vs_, cache_b_k, cache_b_v, j, page_table, bias_sb, lam_vecs,
                                   sub_gain, lambda_init, dec_batch)
                b_kp.append(_heads_last(kt, batch, seq, 2 * N_HEADS, DIFF_DIM))
                b_vp.append(_heads_last(vt, batch, seq, N_HEADS, HEAD_DIM))
                b_ks.append(ks_.reshape(dec_batch, DEC_SEQ, 2 * N_HEADS, DIFF_DIM))
                b_vs.append(vs_.reshape(dec_batch, DEC_SEQ, N_HEADS, HEAD_DIM))
            else:
                o = _stick_prompt(qt, k, vt, batch, seq)
                o_s = _stick_sample(qs, ks_, vs_, cache_c_k, cache_c_v, j, page_table, dec_batch)
                c_kp.append(_heads_last(kt, batch, seq, N_HEADS, HEAD_DIM))
                c_vp.append(_heads_last(vt, batch, seq, N_HEADS, HEAD_DIM))
                c_ks.append(ks_.reshape(dec_batch, DEC_SEQ, N_HEADS, HEAD_DIM))
                c_vs.append(vs_.reshape(dec_batch, DEC_SEQ, N_HEADS, HEAD_DIM))
            y_p = _proj_out([o], gate, y_p, w_out, PROMPT_TM)
            y_s = _proj_out([o_s], gate_s, y_s, w_out, n_s)

    return (y_p.reshape(batch, seq, D_MODEL), y_s.reshape(dec_batch, DEC_SEQ, D_MODEL),
            jnp.stack(a_kp[0]), jnp.stack(a_vp[0]), jnp.stack(a_kp[1]), jnp.stack(a_vp[1]),
            jnp.stack(a_kp[2]), jnp.stack(a_vp[2]), jnp.stack(a_ks), jnp.stack(a_vs),
            jnp.stack(b_kp), jnp.stack(b_vp), jnp.stack(b_ks), jnp.stack(b_vs),
            jnp.stack(c_kp), jnp.stack(c_vp), jnp.stack(c_ks), jnp.stack(c_vs))
```

```python
import functools
import math

import numpy as np
import jax
import jax.numpy as jnp
from jax import lax
from jax.experimental import pallas as pl
from jax.experimental.pallas import tpu as pltpu

F32 = jnp.float32
BF16 = jnp.bfloat16

D_MODEL = 1024
N_HEADS = 24
HEAD_DIM = 64
WIDTH = N_HEADS * HEAD_DIM
H_GROUP = 8
A_OUT = H_GROUP * HEAD_DIM
DIFF_DIM = HEAD_DIM // 2
DILATED_CONFIGS = ((128, 1), (512, 4), (2048, 16))
NUM_BUCKETS = 32
REL_MAX_DIST = 2048
EPS = 1e-6
PAGE_SIZE = 128
DEC_SEQ = 4
NEG = -1e30

LANES = 128
MXU_DIM = 256
VMEM_LIMIT = 52 * 1024 * 1024

BIAS_TABLE_LEN = 8320


def _cparams(sem):
    return pltpu.CompilerParams(dimension_semantics=sem, vmem_limit_bytes=VMEM_LIMIT)


def _nt_dot(a, b):
    return lax.dot_general(a, b, (((1,), (1,)), ((), ())), preferred_element_type=F32)


def _dot(a, b):
    return jnp.dot(a, b, preferred_element_type=F32)


def _split_bf16(x):
    hi = x.astype(BF16)
    lo = (x - hi.astype(F32)).astype(BF16)
    return hi, lo


def _positions_minor(cache):
    n, pool, pos, heads, dim = cache.shape
    return jnp.transpose(cache, (0, 1, 3, 4, 2)).reshape(n * pool, heads * dim, pos)


def _bucket_thresholds():
    max_exact = NUM_BUCKETS // 2
    n = np.arange(1, 1 << 14, dtype=np.float64)
    large = max_exact + (np.log(n / max_exact) / math.log(REL_MAX_DIST / max_exact)
                         * (NUM_BUCKETS - max_exact)).astype(np.int64)
    return tuple(int(n[large >= max_exact + k][0]) for k in range(1, NUM_BUCKETS - max_exact))


_BUCKET_THRESHOLDS = _bucket_thresholds()


def _bias_table_kernel(bt_ref, o_ref):
    max_exact = NUM_BUCKETS // 2
    n = lax.broadcasted_iota(jnp.int32, (N_HEADS, BIAS_TABLE_LEN), 1)
    large = jnp.full(n.shape, max_exact, jnp.int32)
    for thr in _BUCKET_THRESHOLDS:
        large = large + (n >= thr).astype(jnp.int32)
    bucket = jnp.where(n < max_exact, n, jnp.minimum(large, NUM_BUCKETS - 1))
    bt = bt_ref[...]
    acc = jnp.zeros(n.shape, F32)
    for b in range(NUM_BUCKETS):
        acc = jnp.where(bucket == b, jnp.broadcast_to(bt[:, b:b + 1], n.shape), acc)
    o_ref[...] = acc


def _bias_table(rel_bias):
    return pl.pallas_call(
        _bias_table_kernel,
        out_shape=jax.ShapeDtypeStruct((N_HEADS, BIAS_TABLE_LEN), F32),
        name="bias_table",
    )(rel_bias.T)


def _toeplitz(v, n_rows, n_cols):
    h = v.shape[0]
    p = n_rows + n_cols + 1
    pad = jnp.zeros((h, 1), v.dtype)
    w = jnp.concatenate([pad, v[:, ::-1], pad], axis=1)
    x = jnp.tile(w[:, None, :], (1, n_rows, 1)).reshape(h, n_rows * p)
    x = x[:, :n_rows * (p - 1)].reshape(h, n_rows, p - 1)
    return x[:, :, n_rows:n_rows + n_cols]


PROJ_TN = 512


def _group_sumsq(acc, bd, axis):
    hi, lo = _split_bf16(acc * acc)
    parts = []
    for c in range(PROJ_TN // MXU_DIM):
        sl = slice(c * MXU_DIM, (c + 1) * MXU_DIM)
        if axis == 1:
            parts.append(_dot(hi[:, sl], bd) + _dot(lo[:, sl], bd))
        else:
            parts.append(_dot(bd, hi[sl, :]) + _dot(bd, lo[sl, :]))
    return jnp.concatenate(parts, axis=axis)


def _proj_in_kernel(*refs, groups, orient):
    n_w = len(groups)
    n_g = sum(1 for gs in groups if gs)
    x_ref, g_ref = refs[0], refs[1]
    w_refs = refs[2:2 + n_w]
    gain_refs = list(refs[2 + n_w:2 + n_w + n_g])
    bd_ref = refs[2 + n_w + n_g]
    out_refs = list(refs[3 + n_w + n_g:-1])
    h_scr = refs[-1]

    @pl.when(pl.program_id(1) == 0)
    def _():
        x = x_ref[...]
        ms = jnp.mean(x * x, axis=-1, keepdims=True)
        h_scr[...] = (x * lax.rsqrt(ms + EPS) * g_ref[...]).astype(BF16)

    h = h_scr[...]
    for k in range(n_w):
        gs = groups[k]
        if orient[k] == "t":
            acc = _nt_dot(w_refs[k][...], h)
        else:
            acc = _dot(h, w_refs[k][...])
        if gs:
            bd = bd_ref[0 if gs == HEAD_DIM else 1]
            ss = _group_sumsq(acc, bd, 0 if orient[k] == "t" else 1)
            acc = acc * lax.rsqrt(ss * (1.0 / gs) + EPS) * gain_refs.pop(0)[...]
        if orient[k] == "t":
            out_refs.pop(0)[0] = acc
        else:
            out_refs.pop(0)[...] = acc
            if orient[k] == "nt":
                out_refs.pop(0)[0] = acc.T


def _block_diag_ones():
    r = np.arange(MXU_DIM)
    b64 = (r[:, None] // HEAD_DIM == r[None, :] // HEAD_DIM)
    b32 = (r[:, None] // DIFF_DIM == r[None, :] // DIFF_DIM)
    return jnp.asarray(np.stack([b64, b32]).astype(np.float32), dtype=BF16)


def _proj_in(x, g, weights, gains, groups, orient, tm, seq):
    m, d = x.shape
    tn = PROJ_TN
    n_w = len(weights)
    w_dim = weights[0].shape[0] if orient[0] == "t" else weights[0].shape[1]
    spb = seq // tm
    in_specs = [pl.BlockSpec((tm, d), lambda i, j: (i, 0)),
                pl.BlockSpec((1, d), lambda i, j: (0, 0))]
    for k in range(n_w):
        if orient[k] == "t":
            in_specs.append(pl.BlockSpec((tn, d), lambda i, j: (j, 0)))
        else:
            in_specs.append(pl.BlockSpec((d, tn), lambda i, j: (0, j)))
    for k in range(n_w):
        if groups[k]:
            if orient[k] == "t":
                in_specs.append(pl.BlockSpec((tn, 1), lambda i, j: (j, 0)))
            else:
                in_specs.append(pl.BlockSpec((1, tn), lambda i, j: (0, j)))
    in_specs.append(pl.BlockSpec((2, MXU_DIM, MXU_DIM), lambda i, j: (0, 0, 0)))
    out_specs, out_shape = [], []
    for k in range(n_w):
        if "n" in orient[k]:
            out_specs.append(pl.BlockSpec((tm, tn), lambda i, j: (i, j)))
            out_shape.append(jax.ShapeDtypeStruct((m, w_dim), F32))
        if "t" in orient[k]:
            out_specs.append(pl.BlockSpec((1, tn, tm), lambda i, j: (i // spb, j, i % spb)))
            out_shape.append(jax.ShapeDtypeStruct((m // seq, w_dim, seq), F32))
    return pl.pallas_call(
        functools.partial(_proj_in_kernel, groups=tuple(groups), orient=tuple(orient)),
        grid=(m // tm, w_dim // tn),
        in_specs=in_specs,
        out_specs=out_specs,
        out_shape=out_shape,
        scratch_shapes=[pltpu.VMEM((tm, d), BF16)],
        compiler_params=_cparams(("parallel", "arbitrary")),
        name="proj_in",
    )(x, g, *weights, *gains, _block_diag_ones())


def _proj_out_kernel(*refs, merge):
    if merge:
        o0, o1, o2, l0, l1, l2, gate_ref, x_ref, w_ref, out_ref = refs
        la, lb, lc = l0[...], l1[...], l2[...]
        m = jnp.maximum(jnp.maximum(la, lb), lc)
        ea, eb, ec = jnp.exp(la - m), jnp.exp(lb - m), jnp.exp(lc - m)
        o = (ea * o0[...] + eb * o1[...] + ec * o2[...]) / (ea + eb + ec)
    else:
        o_ref, gate_ref, x_ref, w_ref, out_ref = refs
        o = o_ref[...]
    gate = gate_ref[...]
    act = o * (gate / (1.0 + jnp.exp(-gate)))
    out_ref[...] = x_ref[...] + _dot(act.astype(BF16), w_ref[...])


def _proj_out(o_list, gate, x, w, tm):
    m, wd = gate.shape
    d = x.shape[1]
    merge = len(o_list) > 1
    row = lambda i: (i, 0)
    in_specs = [pl.BlockSpec((tm, wd), row) for _ in o_list]
    in_specs += [pl.BlockSpec((tm, wd), row), pl.BlockSpec((tm, d), row),
                 pl.BlockSpec((wd, d), lambda i: (0, 0))]
    return pl.pallas_call(
        functools.partial(_proj_out_kernel, merge=merge),
        grid=(m // tm,),
        in_specs=in_specs,
        out_specs=pl.BlockSpec((tm, d), row),
        out_shape=jax.ShapeDtypeStruct((m, d), F32),
        compiler_params=_cparams(("parallel",)),
        name="proj_out",
    )(*o_list, gate, x, w)


A_BLK = 128


A_HEADS = LANES // HEAD_DIM
A_UNROLL = 4


def _tn_dot(a, b):
    return lax.dot_general(a, b, (((0,), (0,)), ((), ())), preferred_element_type=F32)


def _dil_prompt_kernel(*refs, dil, use_prev):
    if use_prev:
        q_ref, kc_ref, vc_ref, kp_ref, vp_ref, t_ref, o_ref, l_ref = refs
    else:
        q_ref, kc_ref, vc_ref, t_ref, o_ref, l_ref = refs
    scale = HEAD_DIM ** -0.5
    first = jnp.minimum(pl.program_id(2), 1)

    def residues(chains):
        heads = [(i, h) for i in range(len(chains)) for h in range(A_HEADS)]
        rows = [c[0] for c in chains]
        tab = [c[2] for c in chains]
        q2 = [(q_ref[0, rw, :] * scale).astype(BF16) for rw in rows]
        kc2 = [kc_ref[0, rw, :].astype(BF16) for rw in rows]
        sl = [slice(HEAD_DIM * h, HEAD_DIM * (h + 1)) for h in range(A_HEADS)]
        st = [_nt_dot(kc2[i][:, sl[h]], q2[i][:, sl[h]]) + t_ref[tab[i], h, A_BLK:, :] for i, h in heads]
        m = [jnp.max(s, axis=0, keepdims=True) for s in st]
        if use_prev:
            kp2 = [(kp_ref if c[1][0] else kc_ref)[0, c[1][1], :].astype(BF16) for c in chains]
            sp = [_nt_dot(kp2[i][:, sl[h]], q2[i][:, sl[h]]) + t_ref[tab[i], h, :A_BLK, :] for i, h in heads]
            m = [jnp.maximum(a, jnp.max(s, axis=0, keepdims=True)) for a, s in zip(m, sp)]
        e = [jnp.exp(s - a) for s, a in zip(st, m)]
        l = [jnp.sum(x, axis=0, keepdims=True) for x in e]
        vc2 = [vc_ref[0, rw, :].astype(BF16) for rw in rows]
        o = [_tn_dot(vc2[i][:, sl[h]], e[n].astype(BF16)) for n, (i, h) in enumerate(heads)]
        if use_prev:
            ep = [jnp.exp(s - a) for s, a in zip(sp, m)]
            l = [a + jnp.sum(x, axis=0, keepdims=True) for a, x in zip(l, ep)]
            vp2 = [(vp_ref if c[1][0] else vc_ref)[0, c[1][1], :].astype(BF16) for c in chains]
            o = [a + _tn_dot(vp2[i][:, sl[h]], ep[n].astype(BF16)) for n, ((i, h), a) in enumerate(zip(heads, o))]
        o = [a / b for a, b in zip(o, l)]
        lse = [jnp.broadcast_to(a + jnp.log(b), (HEAD_DIM, A_BLK)) for a, b in zip(m, l)]
        for i, rw in enumerate(rows):
            o_ref[0, rw, :] = jnp.concatenate(o[A_HEADS * i:A_HEADS * (i + 1)], axis=0).T
            l_ref[0, rw, :] = jnp.concatenate(lse[A_HEADS * i:A_HEADS * (i + 1)], axis=0).T

    def strided(r):
        rw = pl.ds(r, A_BLK, stride=dil)
        return (rw, (True, rw), first)

    if dil == 1:
        chains = [(pl.ds(0, A_BLK), (True, pl.ds(0, A_BLK)), first)]
        chains += [(pl.ds(A_BLK * j, A_BLK), (False, pl.ds(A_BLK * (j - 1), A_BLK)), 1) for j in range(1, A_UNROLL)]
        residues(chains)
    elif dil <= A_UNROLL:
        residues([strided(r) for r in range(dil)])
    else:
        def body(i, carry):
            residues([strided(i * A_UNROLL + u) for u in range(A_UNROLL)])
            return carry
        lax.fori_loop(0, dil // A_UNROLL, body, 0)


def _group_bias(bias_tab, g):
    window, dil = DILATED_CONFIGS[g]
    n_keys = window // dil + 1
    return bias_tab[g * H_GROUP:(g + 1) * H_GROUP, 0:n_keys * dil:dil]


def _dil_prompt_tables(bias_tab):
    tables = []
    for g in range(len(DILATED_CONFIGS)):
        bj = _group_bias(bias_tab, g)
        lo, hi = A_BLK - (2 * A_BLK - 1), A_BLK + A_BLK - 1
        v = jnp.full((H_GROUP, hi - lo + 1), NEG, F32)
        v = lax.dynamic_update_slice(v, bj, (0, -lo))
        t_gen = _toeplitz(v, A_BLK, 2 * A_BLK)
        t_first = t_gen.at[:, :, :A_BLK].set(NEG)
        tables.append(jnp.swapaxes(jnp.stack([t_first, t_gen]), 2, 3))
    return tables


def _dil_prompt_group(q, k, v, table, g, batch, seq):
    dil = DILATED_CONFIGS[g][1]
    rows = A_BLK * (dil if dil > 1 else A_UNROLL)
    prev_rows = rows if dil > 1 else A_BLK
    nqb = seq // rows
    use_prev = nqb > 1
    n_hp = A_OUT // LANES
    q3, k3, v3 = (a.reshape(batch, seq, WIDTH) for a in (q, k, v))
    cur = lambda b, h, i: (b, i, g * n_hp + h)
    prev = lambda b, h, i: (b, jnp.maximum(i * (rows // prev_rows) - 1, 0), g * n_hp + h)
    blk = (1, rows, LANES)
    in_specs = [pl.BlockSpec(blk, cur), pl.BlockSpec(blk, cur), pl.BlockSpec(blk, cur)]
    args = [q3, k3, v3]
    if use_prev:
        in_specs += [pl.BlockSpec((1, prev_rows, LANES), prev)] * 2
        args += [k3, v3]
    in_specs += [pl.BlockSpec((2, A_HEADS, 2 * A_BLK, A_BLK), lambda b, h, i: (0, h, 0, 0))]
    args += [table]
    out_map = lambda b, h, i: (b, i, h)
    o, lse = pl.pallas_call(
        functools.partial(_dil_prompt_kernel, dil=dil, use_prev=use_prev),
        grid=(batch, n_hp, nqb),
        in_specs=in_specs,
        out_specs=[pl.BlockSpec(blk, out_map), pl.BlockSpec(blk, out_map)],
        out_shape=[jax.ShapeDtypeStruct((batch, seq, A_OUT), F32)] * 2,
        compiler_params=_cparams(("parallel", "parallel", "arbitrary")),
        name="dilated_prompt_g%d" % g,
    )(*args)
    return o.reshape(batch * seq, A_OUT), lse.reshape(batch * seq, A_OUT)


NEW_PAD = 16
A_ROWS = H_GROUP * DEC_SEQ


def _dil_sample_kernel(qbd_ref, kn_ref, vn_ref, k0, v0, k1, v1, k2, v2, tp0, tp1, tp2, tn_ref, o_ref):
    scale = HEAD_DIM ** -0.5
    outs, lses = [], []
    for g, (kr, vr, tp) in enumerate(((k0, v0, tp0), (k1, v1, tp1), (k2, v2, tp2))):
        gs = slice(A_OUT * g, A_OUT * (g + 1))
        qb = qbd_ref[0, g]
        s = _dot(qb, kr[0].astype(BF16)) * scale + tp[...]
        sn = _nt_dot(qb, kn_ref[0, :, gs].astype(BF16)) * scale + tn_ref[g]
        m = jnp.maximum(jnp.max(s, axis=1, keepdims=True), jnp.max(sn, axis=1, keepdims=True))
        e = jnp.exp(s - m)
        en = jnp.exp(sn - m)
        l = jnp.sum(e, axis=1, keepdims=True) + jnp.sum(en, axis=1, keepdims=True)
        o = _nt_dot(e.astype(BF16), vr[0].astype(BF16)) + _dot(en.astype(BF16), vn_ref[0, :, gs].astype(BF16))
        outs.append(o / l)
        lses.append(m + jnp.log(l))
    mm = jnp.maximum(jnp.maximum(lses[0], lses[1]), lses[2])
    es = [jnp.exp(x - mm) for x in lses]
    o_ref[0] = (es[0] * outs[0] + es[1] * outs[1] + es[2] * outs[2]) / (es[0] + es[1] + es[2])


def _dil_sample_tables(bias_tab):
    tps, tns = [], []
    tq = np.arange(DEC_SEQ)
    for g, (window, dil) in enumerate(DILATED_CONFIGS):
        bg = bias_tab[g * H_GROUP:(g + 1) * H_GROUP]
        length = window
        pos = np.arange(length)
        dist = length + tq[:, None] - pos[None, :]
        ok = (dist % dil == 0) & (dist <= window)
        rows = jnp.stack([bg[:, t + 1:t + 1 + length][:, ::-1] for t in range(DEC_SEQ)], axis=1)
        tps.append(jnp.where(jnp.asarray(ok)[None], rows, NEG).reshape(A_ROWS, length))
        tn_ = np.arange(NEW_PAD)
        dn = tq[:, None] - tn_[None, :]
        okn = (dn >= 0) & (tn_[None, :] < DEC_SEQ) & (dn % dil == 0)
        vals = bg[:, np.clip(dn, 0, DEC_SEQ)]
        tns.append(jnp.where(jnp.asarray(okn)[None], vals, NEG).reshape(A_ROWS, NEW_PAD))
    return tps, jnp.stack(tns)


def _pad_new(x, dec_batch, rows):
    x = x.reshape(dec_batch, DEC_SEQ, x.shape[-1])
    return jnp.pad(x, ((0, 0), (0, rows - DEC_SEQ), (0, 0)))


def _dil_sample(q, k, v, caches_k, caches_v, j, tps, tn, dec_batch):
    n_grp = len(DILATED_CONFIGS)
    sel = np.zeros((H_GROUP, DEC_SEQ, A_ROWS), np.float32)
    for h in range(H_GROUP):
        for t in range(DEC_SEQ):
            sel[h, t, h * DEC_SEQ + t] = 1.0
    q5 = q.reshape(dec_batch, DEC_SEQ, n_grp, H_GROUP, HEAD_DIM)
    qbd = jnp.einsum("btghd,htr->bgrhd", q5, jnp.asarray(sel), precision=lax.Precision.HIGHEST)
    qbd = qbd.reshape(dec_batch, n_grp, A_ROWS, A_OUT).astype(BF16)
    kn, vn = (_pad_new(a, dec_batch, NEW_PAD) for a in (k, v))
    args = [qbd, kn, vn]
    in_specs = [pl.BlockSpec((1, n_grp, A_ROWS, A_OUT), lambda b: (b, 0, 0, 0)),
                pl.BlockSpec((1, NEW_PAD, WIDTH), lambda b: (b, 0, 0)),
                pl.BlockSpec((1, NEW_PAD, WIDTH), lambda b: (b, 0, 0))]
    for g, (window, dil) in enumerate(DILATED_CONFIGS):
        for c in (caches_k[g], caches_v[g]):
            args.append(_positions_minor(c))
            in_specs.append(pl.BlockSpec((1, A_OUT, window), lambda b, j=j: (j * dec_batch + b, 0, 0)))
    for tp in tps:
        args.append(tp)
        in_specs.append(pl.BlockSpec(tp.shape, lambda b: (0, 0)))
    args.append(tn)
    in_specs.append(pl.BlockSpec(tn.shape, lambda b: (0, 0, 0)))
    o = pl.pallas_call(
        _dil_sample_kernel,
        grid=(dec_batch,),
        in_specs=in_specs,
        out_specs=pl.BlockSpec((1, A_ROWS, A_OUT), lambda b: (b, 0, 0)),
        out_shape=jax.ShapeDtypeStruct((dec_batch, A_ROWS, A_OUT), F32),
        compiler_params=_cparams(("parallel",)),
        name="dilated_sample",
    )(*args)
    o = o.reshape(dec_batch, H_GROUP, DEC_SEQ, H_GROUP, HEAD_DIM)
    idx = np.arange(H_GROUP)
    o = o[:, idx, :, idx, :]
    return jnp.transpose(o, (1, 2, 0, 3)).reshape(dec_batch * DEC_SEQ, A_OUT)


BC_BLK = 256
HEADS_PER_STEP = MXU_DIM // HEAD_DIM


def _lambda_value(lam_ref, lambda_init):
    lv = lam_ref[...]
    s1 = jnp.sum(lv[0:1] * lv[1:2], axis=1, keepdims=True)
    s2 = jnp.sum(lv[2:3] * lv[3:4], axis=1, keepdims=True)
    return jnp.exp(s1) - jnp.exp(s2) + lambda_init


LOG2E = 1.4426950408889634


def _diff_prompt_kernel(qt_ref, k_ref, vt_ref, ext_ref, lam_ref, sg_ref, o_ref, tab_scr, st_scr, p_scr, *,
                        lambda_init, nqb):
    b = pl.program_id(1)
    qi = pl.program_id(2)
    scale2 = DIFF_DIM ** -0.5 * LOG2E

    @pl.when((b == 0) & (qi == 0))
    def _():
        for hl in range(HEADS_PER_STEP):
            for dd in range(nqb):
                row = ext_ref[0, hl:hl + 1, dd * BC_BLK:(dd + 2) * BC_BLK] * LOG2E
                x = jnp.broadcast_to(row, (BC_BLK, 2 * BC_BLK))
                tab_scr[hl, dd] = pltpu.roll(x, 0, 1, stride=1, stride_axis=0)[:, BC_BLK:]

    streams = [(hl, half) for hl in range(HEADS_PER_STEP) for half in range(2)]
    qts = []
    for hl, half in streams:
        c0 = HEAD_DIM * hl + DIFF_DIM * half
        qts.append((qt_ref[0, c0:c0 + DIFF_DIM, :] * scale2).astype(BF16))

    def body(kb, carry):
        return update(carry, [kb])

    def update(carry, kbs):
        ms, ls, accs = carry
        keys = [pl.ds(pl.multiple_of(kb * BC_BLK, BC_BLK), BC_BLK) for kb in kbs]
        ms_n, ls_n, accs_n, alphas = [], [], [], []
        for s, (hl, half) in enumerate(streams):
            c0 = HEAD_DIM * hl + DIFF_DIM * half
            m = ms[s]
            for j, kb in enumerate(kbs):
                ks = k_ref[0, c0:c0 + DIFF_DIM, keys[j]].astype(BF16)
                st = _tn_dot(ks, qts[s]) + tab_scr[hl, qi - kb]
                st_scr[j, s] = st
                m = jnp.maximum(m, jnp.max(st, axis=0, keepdims=True))
            ms_n.append(m)
        for s in range(len(streams)):
            alphas.append(jnp.exp2(ms[s] - ms_n[s]))
            l = alphas[s] * ls[s]
            for j in range(len(kbs)):
                p = jnp.exp2(st_scr[j, s] - ms_n[s])
                l = l + jnp.sum(p, axis=0, keepdims=True)
                p_scr[j, s] = p.astype(BF16)
            ls_n.append(l)
        for s, (hl, half) in enumerate(streams):
            acc = alphas[s] * accs[s]
            for j in range(len(kbs)):
                vt = vt_ref[0, HEAD_DIM * hl:HEAD_DIM * (hl + 1), keys[j]].astype(BF16)
                acc = acc + _dot(vt, p_scr[j, s])
            accs_n.append(acc)
        return tuple(ms_n), tuple(ls_n), tuple(accs_n)

    n_s = len(streams)
    init = (tuple(jnp.full((1, BC_BLK), NEG, F32) for _ in range(n_s)),
            tuple(jnp.zeros((1, BC_BLK), F32) for _ in range(n_s)),
            tuple(jnp.zeros((HEAD_DIM, BC_BLK), F32) for _ in range(n_s)))
    n_kb = qi + 1
    carry = lax.fori_loop(0, n_kb // 2, lambda i, c: update(c, [2 * i, 2 * i + 1]), init)
    _, ls, accs = lax.cond(n_kb % 2 == 1, lambda c: body(qi, c), lambda c: c, carry)

    lam = _lambda_value(lam_ref, lambda_init)
    outs = []
    for hl in range(HEADS_PER_STEP):
        o = accs[2 * hl] / ls[2 * hl] - lam * (accs[2 * hl + 1] / ls[2 * hl + 1])
        ms = jnp.mean(o * o, axis=0, keepdims=True)
        outs.append(o * lax.rsqrt(ms + EPS) * sg_ref[...] * (1.0 - lambda_init))
    o_ref[0] = jnp.concatenate(outs, axis=0).T


def _diff_prompt_bias_rows(bias_tab, seq):
    ext = jnp.concatenate([jnp.full((N_HEADS, BC_BLK), NEG, F32), bias_tab[:, :seq + BC_BLK]], axis=1)
    return ext.reshape(N_HEADS // HEADS_PER_STEP, HEADS_PER_STEP, seq + 2 * BC_BLK)


def _diff_prompt(qt, kt, vt, ext, lam_vecs, sub_gain_col, lambda_init, batch, seq):
    nqb = seq // BC_BLK
    n_hg = WIDTH // MXU_DIM
    o = pl.pallas_call(
        functools.partial(_diff_prompt_kernel, lambda_init=lambda_init, nqb=nqb),
        grid=(n_hg, batch, nqb),
        in_specs=[pl.BlockSpec((1, MXU_DIM, BC_BLK), lambda h, b, i: (b, h, i)),
                  pl.BlockSpec((1, MXU_DIM, seq), lambda h, b, i: (b, h, 0)),
                  pl.BlockSpec((1, MXU_DIM, seq), lambda h, b, i: (b, h, 0)),
                  pl.BlockSpec((1, HEADS_PER_STEP, seq + 2 * BC_BLK), lambda h, b, i: (h, 0, 0)),
                  pl.BlockSpec((4, DIFF_DIM), lambda h, b, i: (0, 0)),
                  pl.BlockSpec((HEAD_DIM, 1), lambda h, b, i: (0, 0))],
        out_specs=pl.BlockSpec((1, BC_BLK, MXU_DIM), lambda h, b, i: (b, i, h)),
        out_shape=jax.ShapeDtypeStruct((batch, seq, WIDTH), F32),
        scratch_shapes=[pltpu.VMEM((HEADS_PER_STEP, nqb, BC_BLK, BC_BLK), F32),
                        pltpu.VMEM((2, 2 * HEADS_PER_STEP, BC_BLK, BC_BLK), F32),
                        pltpu.VMEM((2, 2 * HEADS_PER_STEP, BC_BLK, BC_BLK), BF16)],
        compiler_params=_cparams(("arbitrary", "arbitrary", "arbitrary")),
        name="diff_prompt",
    )(qt, kt, vt, ext, lam_vecs, sub_gain_col)
    return o.reshape(batch * seq, WIDTH)


def _log_sigmoid(z):
    return jnp.minimum(z, 0.0) - jnp.log(1.0 + jnp.exp(-jnp.abs(z)))


def _suffix_sum_rows(x, tri):
    hi, lo = _split_bf16(x)
    return _dot(hi, tri) + _dot(lo, tri)


STICK_HEADS = MXU_DIM // HEAD_DIM
UNDERFLOW = 104.0


def _stick_prompt_kernel(qt_ref, k_ref, vt_ref, tri_ref, o_ref, lsc_scr, hilo_scr, a_scr):
    qi = pl.program_id(2)
    scale = HEAD_DIM ** -0.5
    tri = tri_ref[...]
    key_i = lax.broadcasted_iota(jnp.int32, (BC_BLK, BC_BLK), 0)
    qry_i = lax.broadcasted_iota(jnp.int32, (BC_BLK, BC_BLK), 1)
    strict = key_i < qry_i
    qts = [(qt_ref[0, HEAD_DIM * h:HEAD_DIM * (h + 1), :] * scale).astype(BF16) for h in range(STICK_HEADS)]

    def block(kb, cs, accs, masked):
        keys = pl.ds(pl.multiple_of(kb * BC_BLK, BC_BLK), BC_BLK)
        cs_n, accs_n, edge = [], [], []

        def stage_a(h):
            fs = slice(HEAD_DIM * h, HEAD_DIM * (h + 1))
            z = _tn_dot(k_ref[0, fs, keys].astype(BF16), qts[h])
            neg_abs = lax.bitcast_convert_type(
                lax.bitcast_convert_type(z, jnp.uint32) | jnp.uint32(0x80000000), F32)
            t = jnp.log(1.0 + jnp.exp(neg_abs))
            ls = jnp.minimum(z, 0.0) - t
            stay = ls - z
            if masked:
                stay = jnp.where(strict, stay, 0.0)
            hi, lo = _split_bf16(stay)
            hilo_scr[h, :BC_BLK] = hi
            hilo_scr[h, BC_BLK:] = lo
            lsc_scr[h] = ls + cs[h]
            edge.append(stay[0:1] - ls[0:1])

        def stage_b(h):
            x = lsc_scr[h] + _dot(tri, hilo_scr[h])
            a = jnp.exp(x)
            if masked:
                a = jnp.where(strict, a, 0.0)
            a_scr[h] = a.astype(BF16)
            cs_n.append(x[0:1] + edge[h])

        def stage_c(h):
            fs = slice(HEAD_DIM * h, HEAD_DIM * (h + 1))
            accs_n.append(accs[h] + _dot(vt_ref[0, fs, keys].astype(BF16), a_scr[h]))

        for stage in (stage_a, stage_b, stage_c):
            for h in range(STICK_HEADS):
                stage(h)
        return tuple(cs_n), tuple(accs_n)

    def alive(cs):
        top = cs[0]
        for c in cs[1:]:
            top = jnp.maximum(top, c)
        return jnp.max(top) > -UNDERFLOW

    cs = tuple(jnp.zeros((1, BC_BLK), F32) for _ in range(STICK_HEADS))
    accs = tuple(jnp.zeros((HEAD_DIM, BC_BLK), F32) for _ in range(STICK_HEADS))
    cs, accs = block(qi, cs, accs, True)

    def cond(carry):
        kb, go, _, _ = carry
        return (kb >= 0) & go

    def body(carry):
        kb, _, cs, accs = carry
        cs, accs = block(kb, cs, accs, False)
        return kb - 1, alive(cs), cs, accs

    _, _, _, accs = lax.while_loop(cond, body, (qi - 1, alive(cs), cs, accs))
    o_ref[0] = jnp.concatenate(accs, axis=0).T


def _tri_later(n):
    r = np.arange(n)
    return jnp.asarray((r[:, None] > r[None, :]).astype(np.float32), dtype=BF16)


def _stick_prompt(qt, kt, vt, batch, seq):
    nqb = seq // BC_BLK
    n_hg = WIDTH // MXU_DIM
    o = pl.pallas_call(
        _stick_prompt_kernel,
        grid=(n_hg, batch, nqb),
        in_specs=[pl.BlockSpec((1, MXU_DIM, BC_BLK), lambda h, b, i: (b, h, i)),
                  pl.BlockSpec((1, MXU_DIM, seq), lambda h, b, i: (b, h, 0)),
                  pl.BlockSpec((1, MXU_DIM, seq), lambda h, b, i: (b, h, 0)),
                  pl.BlockSpec((BC_BLK, 2 * BC_BLK), lambda h, b, i: (0, 0))],
        out_specs=pl.BlockSpec((1, BC_BLK, MXU_DIM), lambda h, b, i: (b, i, h)),
        out_shape=jax.ShapeDtypeStruct((batch, seq, WIDTH), F32),
        scratch_shapes=[pltpu.VMEM((STICK_HEADS, BC_BLK, BC_BLK), F32),
                        pltpu.VMEM((STICK_HEADS, 2 * BC_BLK, BC_BLK), BF16),
                        pltpu.VMEM((STICK_HEADS, BC_BLK, BC_BLK), BF16)],
        compiler_params=_cparams(("parallel", "parallel", "arbitrary")),
        name="stick_prompt",
    )(qt, kt, vt, jnp.tile(_tri_later(BC_BLK).T, (1, 2)))
    return o.reshape(batch * seq, WIDTH)


HEADS_PER_GROUP = MXU_DIM // HEAD_DIM
N_VGROUPS = WIDTH // MXU_DIM
B_ROWS = 2 * HEADS_PER_GROUP * DEC_SEQ
C_ROWS = HEADS_PER_GROUP * DEC_SEQ
B_QROWS = N_VGROUPS * B_ROWS
C_QROWS = N_VGROUPS * C_ROWS
PAGES_PER_STEP = 8


def _page_scores(q_ref, k_page_ref):
    parts = []
    for c in range(N_VGROUPS):
        kt = k_page_ref[0, MXU_DIM * c:MXU_DIM * (c + 1), :].astype(BF16)
        parts.append(_dot(q_ref[0, c], kt))
    return jnp.concatenate(parts, axis=0)


def _page_values(w, v_page_ref, rows):
    outs = []
    for c in range(N_VGROUPS):
        vt = v_page_ref[0, MXU_DIM * c:MXU_DIM * (c + 1), :].astype(BF16)
        outs.append(_nt_dot(w[rows * c:rows * (c + 1), :].astype(BF16), vt))
    return outs


def _diff_sample_kernel(pt_ref, q_ref, *refs, lambda_init, n_steps):
    k_refs = refs[:PAGES_PER_STEP]
    v_refs = refs[PAGES_PER_STEP:2 * PAGES_PER_STEP]
    (kn_ref, vn_ref, bias_ref, biasn_ref, lam_ref, sg_ref, o_ref,
     s_scr, m_scr, l_scr, acc_scr) = refs[2 * PAGES_PER_STEP:]
    ph = pl.program_id(1)
    p = pl.program_id(2)
    scale = DIFF_DIM ** -0.5
    n_pages = n_steps * PAGES_PER_STEP

    def score_pages(pages):
        m = m_scr[...]
        for slot, k_ref, bias in pages:
            s = _page_scores(q_ref, k_ref) * scale + bias
            s_scr[slot] = s
            m = jnp.maximum(m, s)
        m_scr[...] = m

    def value_pages(pages):
        l = l_scr[...]
        accs = [acc_scr[c] for c in range(N_VGROUPS)]
        for slot, v_ref in pages:
            e = jnp.exp(s_scr[slot] - m_scr[...])
            l = l + e
            for c, d in enumerate(_page_values(e, v_ref, B_ROWS)):
                accs[c] = accs[c] + d
        l_scr[...] = l
        for c in range(N_VGROUPS):
            acc_scr[c] = accs[c]

    @pl.when((ph == 0) & (p == 0))
    def _():
        m_scr[...] = jnp.full(m_scr.shape, NEG, F32)
        l_scr[...] = jnp.zeros(l_scr.shape, F32)
        acc_scr[...] = jnp.zeros(acc_scr.shape, F32)

    @pl.when(ph == 0)
    def _():
        score_pages([(p * PAGES_PER_STEP + g, k_refs[g], bias_ref[:, PAGE_SIZE * g:PAGE_SIZE * (g + 1)])
                     for g in range(PAGES_PER_STEP)])

    @pl.when((ph == 0) & (p == n_steps - 1))
    def _():
        score_pages([(n_pages, kn_ref, biasn_ref[...])])

    @pl.when((ph == 1) & (p == 0))
    def _():
        m_scr[...] = jnp.broadcast_to(jnp.max(m_scr[...], axis=1, keepdims=True), m_scr.shape)

    @pl.when(ph == 1)
    def _():
        value_pages([(p * PAGES_PER_STEP + g, v_refs[g]) for g in range(PAGES_PER_STEP)])

    @pl.when((ph == 1) & (p == n_steps - 1))
    def _():
        value_pages([(n_pages, vn_ref)])
        lam = _lambda_value(lam_ref, lambda_init)
        l = jnp.sum(l_scr[...], axis=1, keepdims=True)
        half_rows = B_ROWS // 2
        r = lax.broadcasted_iota(jnp.int32, (half_rows, MXU_DIM), 0)
        ln = lax.broadcasted_iota(jnp.int32, (half_rows, MXU_DIM), 1)
        own = (r // DEC_SEQ) == (ln // HEAD_DIM)
        for c in range(N_VGROUPS):
            an = acc_scr[c] / l[B_ROWS * c:B_ROWS * (c + 1)]
            d = jnp.where(own, an[:half_rows] - lam * an[half_rows:], 0.0)
            ms = jnp.sum(d * d, axis=1, keepdims=True) * (1.0 / HEAD_DIM)
            o_ref[0, c] = d * lax.rsqrt(ms + EPS) * sg_ref[...] * (1.0 - lambda_init)


def _diag_blocks(x, dec_batch):
    x = x.reshape(dec_batch, N_VGROUPS, HEADS_PER_GROUP, DEC_SEQ, HEADS_PER_GROUP, HEAD_DIM)
    idx = np.arange(HEADS_PER_GROUP)
    x = x[:, :, idx, :, idx, :]
    x = jnp.transpose(x, (1, 3, 2, 0, 4))
    return x.reshape(dec_batch * DEC_SEQ, WIDTH)


def _query_tiles(q, dec_batch, sub_dim, rows, row_of):
    n_sub = MXU_DIM // sub_dim
    sel = np.zeros((n_sub, DEC_SEQ, rows), np.float32)
    for j in range(n_sub):
        for t in range(DEC_SEQ):
            sel[j, t, row_of(j, t)] = 1.0
    q5 = q.reshape(dec_batch, DEC_SEQ, N_VGROUPS, n_sub, sub_dim)
    qt = jnp.einsum("btcjd,jtr->bcrjd", q5, jnp.asarray(sel), precision=lax.Precision.HIGHEST)
    return qt.reshape(dec_batch, N_VGROUPS, rows, MXU_DIM).astype(BF16)


def _b_row(j, t):
    return (j % 2) * (B_ROWS // 2) + (j // 2) * DEC_SEQ + t


def _c_row(hl, t):
    return hl * DEC_SEQ + t


def _diff_sample_bias(bias_tab, past_len):
    segs = [bias_tab[:, t + 1:t + 1 + past_len][:, ::-1] for t in range(DEC_SEQ)]
    past = jnp.stack(segs, axis=1)
    tn_ = np.arange(PAGE_SIZE)[None, :]
    tq = np.arange(DEC_SEQ)[:, None]
    dist = tq - tn_
    ok = (dist >= 0) & (tn_ < DEC_SEQ)
    new = jnp.where(jnp.asarray(ok)[None], bias_tab[:, np.clip(dist, 0, DEC_SEQ)], NEG)
    full = jnp.concatenate([past, new], axis=2)
    n_cols = past_len + PAGE_SIZE
    full = full.reshape(N_VGROUPS, 1, HEADS_PER_GROUP, DEC_SEQ, n_cols)
    full = jnp.broadcast_to(full, (N_VGROUPS, 2, HEADS_PER_GROUP, DEC_SEQ, n_cols))
    return full.reshape(B_QROWS, n_cols)


def _new_page(x, dec_batch):
    x = jnp.transpose(x.reshape(dec_batch, DEC_SEQ, WIDTH), (0, 2, 1))
    return jnp.pad(x, ((0, 0), (0, 0), (0, PAGE_SIZE - DEC_SEQ)))


def _diff_sample(q, k, v, cache_k, cache_v, j, page_table, bias_sb, lam_vecs, sub_gain, lambda_init, dec_batch):
    n_pool = cache_k.shape[1]
    n_pages = page_table.shape[1]
    n_steps = n_pages // PAGES_PER_STEP
    ck = _positions_minor(cache_k)
    cv = _positions_minor(cache_v)
    qt = _query_tiles(q, dec_batch, DIFF_DIM, B_ROWS, _b_row)
    kn = _new_page(k, dec_batch)
    vn = _new_page(v, dec_batch)
    sg = jnp.tile(sub_gain, (1, HEADS_PER_GROUP))
    last = n_steps - 1
    page = (1, WIDTH, PAGE_SIZE)

    def k_map(g):
        return lambda b, ph, p, pt: (j * n_pool + pt[b, jnp.where(ph == 0, p, last) * PAGES_PER_STEP + g], 0, 0)

    def v_map(g):
        return lambda b, ph, p, pt: (j * n_pool + pt[b, jnp.where(ph == 1, p, 0) * PAGES_PER_STEP + g], 0, 0)

    per_b = lambda b, ph, p, pt: (b, 0, 0)
    in_specs = [pl.BlockSpec((1, N_VGROUPS, B_ROWS, MXU_DIM), lambda b, ph, p, pt: (b, 0, 0, 0))]
    in_specs += [pl.BlockSpec(page, k_map(g)) for g in range(PAGES_PER_STEP)]
    in_specs += [pl.BlockSpec(page, v_map(g)) for g in range(PAGES_PER_STEP)]
    in_specs += [pl.BlockSpec(page, per_b), pl.BlockSpec(page, per_b),
                 pl.BlockSpec((B_QROWS, PAGES_PER_STEP * PAGE_SIZE),
                              lambda b, ph, p, pt: (0, jnp.where(ph == 0, p, last))),
                 pl.BlockSpec((B_QROWS, PAGE_SIZE), lambda b, ph, p, pt: (0, n_pages)),
                 pl.BlockSpec((4, DIFF_DIM), lambda b, ph, p, pt: (0, 0)),
                 pl.BlockSpec((1, MXU_DIM), lambda b, ph, p, pt: (0, 0))]
    grid_spec = pltpu.PrefetchScalarGridSpec(
        num_scalar_prefetch=1,
        grid=(dec_batch, 2, n_steps),
        in_specs=in_specs,
        out_specs=pl.BlockSpec((1, N_VGROUPS, B_ROWS // 2, MXU_DIM), lambda b, ph, p, pt: (b, 0, 0, 0)),
        scratch_shapes=[pltpu.VMEM((n_pages + 1, B_QROWS, PAGE_SIZE), F32),
                        pltpu.VMEM((B_QROWS, PAGE_SIZE), F32),
                        pltpu.VMEM((B_QROWS, PAGE_SIZE), F32),
                        pltpu.VMEM((N_VGROUPS, B_ROWS, MXU_DIM), F32)])
    o = pl.pallas_call(
        functools.partial(_diff_sample_kernel, lambda_init=lambda_init, n_steps=n_steps),
        grid_spec=grid_spec,
        out_shape=jax.ShapeDtypeStruct((dec_batch, N_VGROUPS, B_ROWS // 2, MXU_DIM), F32),
        compiler_params=_cparams(("parallel", "arbitrary", "arbitrary")),
        name="diff_sample",
    )(page_table, qt, *([ck] * PAGES_PER_STEP), *([cv] * PAGES_PER_STEP), kn, vn, bias_sb, bias_sb, lam_vecs, sg)
    return _diag_blocks(o, dec_batch)


def _stick_sample_kernel(pt_ref, q_ref, *refs, n_steps):
    k_refs = refs[:PAGES_PER_STEP]
    v_refs = refs[PAGES_PER_STEP:2 * PAGES_PER_STEP]
    kn_ref, vn_ref, valid_ref, tri_ref, o_ref, c_scr, acc_scr = refs[2 * PAGES_PER_STEP:]
    p = pl.program_id(1)
    scale = HEAD_DIM ** -0.5

    def pages(items):
        c = c_scr[...]
        accs = [acc_scr[g] for g in range(N_VGROUPS)]
        parts = []
        for k_ref, v_ref, valid in items:
            z = _page_scores(q_ref, k_ref) * scale
            ls = _log_sigmoid(z)
            stay = ls - z
            if valid is not None:
                stay = stay * valid
            parts.append((ls, stay, jnp.sum(stay, axis=1, keepdims=True)))
        for (k_ref, v_ref, valid), (ls, stay, total) in zip(items, parts):
            after = _suffix_sum_rows(stay, tri_ref[...])
            a = jnp.exp(ls + after + c)
            if valid is not None:
                a = a * valid
            c = c + total
            for g, d in enumerate(_page_values(a, v_ref, C_ROWS)):
                accs[g] = accs[g] + d
        c_scr[...] = c
        for g in range(N_VGROUPS):
            acc_scr[g] = accs[g]

    @pl.when(p == 0)
    def _():
        c_scr[...] = jnp.zeros(c_scr.shape, F32)
        acc_scr[...] = jnp.zeros(acc_scr.shape, F32)
        pages([(kn_ref, vn_ref, valid_ref[...])])

    pages([(k_refs[g], v_refs[g], None) for g in range(PAGES_PER_STEP)])

    @pl.when(p == n_steps - 1)
    def _():
        o_ref[0] = acc_scr[...]


def _stick_sample(q, k, v, cache_k, cache_v, j, page_table, dec_batch):
    n_pool = cache_k.shape[1]
    n_pages = page_table.shape[1]
    n_steps = n_pages // PAGES_PER_STEP
    ck = _positions_minor(cache_k)
    cv = _positions_minor(cache_v)
    qt = _query_tiles(q, dec_batch, HEAD_DIM, C_ROWS, _c_row)
    kn = _new_page(k, dec_batch)
    vn = _new_page(v, dec_batch)
    valid = np.zeros((C_QROWS, PAGE_SIZE), np.float32)
    for c in range(N_VGROUPS):
        for hl in range(HEADS_PER_GROUP):
            for t in range(DEC_SEQ):
                valid[c * C_ROWS + _c_row(hl, t), :t] = 1.0
    page = (1, WIDTH, PAGE_SIZE)

    def page_map(g):
        return lambda b, p, pt: (j * n_pool + pt[b, n_pages - 1 - (p * PAGES_PER_STEP + g)], 0, 0)

    per_b = lambda b, p, pt: (b, 0, 0)
    in_specs = [pl.BlockSpec((1, N_VGROUPS, C_ROWS, MXU_DIM), lambda b, p, pt: (b, 0, 0, 0))]
    in_specs += [pl.BlockSpec(page, page_map(g)) for g in range(PAGES_PER_STEP)] * 2
    in_specs += [pl.BlockSpec(page, per_b), pl.BlockSpec(page, per_b),
                 pl.BlockSpec((C_QROWS, PAGE_SIZE), lambda b, p, pt: (0, 0)),
                 pl.BlockSpec((PAGE_SIZE, PAGE_SIZE), lambda b, p, pt: (0, 0))]
    grid_spec = pltpu.PrefetchScalarGridSpec(
        num_scalar_prefetch=1,
        grid=(dec_batch, n_steps),
        in_specs=in_specs,
        out_specs=pl.BlockSpec((1, N_VGROUPS, C_ROWS, MXU_DIM), lambda b, p, pt: (b, 0, 0, 0)),
        scratch_shapes=[pltpu.VMEM((C_QROWS, 1), F32),
                        pltpu.VMEM((N_VGROUPS, C_ROWS, MXU_DIM), F32)])
    o = pl.pallas_call(
        functools.partial(_stick_sample_kernel, n_steps=n_steps),
        grid_spec=grid_spec,
        out_shape=jax.ShapeDtypeStruct((dec_batch, N_VGROUPS, C_ROWS, MXU_DIM), F32),
        compiler_params=_cparams(("parallel", "arbitrary")),
        name="stick_sample",
    )(page_table, qt, *([ck] * PAGES_PER_STEP), *([cv] * PAGES_PER_STEP), kn, vn, jnp.asarray(valid),
      _tri_later(PAGE_SIZE))
    return _diag_blocks(o, dec_batch)


PROMPT_TM = 1024


def _heads_last(xt, batch, seq, n_sub, sub_dim):
    return jnp.transpose(xt.reshape(batch, n_sub, sub_dim, seq), (0, 3, 1, 2))


def kernel(x_prompt, x_sample, cache_a_k_g0, cache_a_v_g0, cache_a_k_g1, cache_a_v_g1, cache_a_k_g2, cache_a_v_g2, cache_b_k, cache_b_v, cache_c_k, cache_c_v, page_table, norm_gain, rel_bias, w_in_a, w_out_a, q_gain_a, k_gain_a, w_in_b, w_out_b, q_gain_b, k_gain_b, lambda_q1, lambda_k1, lambda_q2, lambda_k2, sub_gain_b, w_in_c, w_out_c):
    batch, seq, _ = x_prompt.shape
    dec_batch = x_sample.shape[0]
    depth = norm_gain.shape[0]
    n_p = batch * seq
    n_s = dec_batch * DEC_SEQ
    caches_ak = (cache_a_k_g0, cache_a_k_g1, cache_a_k_g2)
    caches_av = (cache_a_v_g0, cache_a_v_g1, cache_a_v_g2)

    bias_tab = _bias_table(rel_bias)
    tabs_a = _dil_prompt_tables(bias_tab)
    tps_a, tn_a = _dil_sample_tables(bias_tab)
    ext_b = _diff_prompt_bias_rows(bias_tab, seq)
    bias_sb = _diff_sample_bias(bias_tab, page_table.shape[1] * PAGE_SIZE)

    y_p = x_prompt.reshape(n_p, D_MODEL)
    y_s = x_sample.reshape(n_s, D_MODEL)
    a_kp = [[] for _ in DILATED_CONFIGS]
    a_vp = [[] for _ in DILATED_CONFIGS]
    a_ks, a_vs = [], []
    b_kp, b_vp, b_ks, b_vs = [], [], [], []
    c_kp, c_vp, c_ks, c_vs = [], [], [], []
    no_t = ("n", "n", "n", "n")

    for i in range(depth):
        kind, j = i % 3, i // 3
        g = norm_gain[i][None, :]
        if kind == 0:
            w = w_in_a[j].astype(BF16)
            wq, wk, wv, wg = (w[:, :WIDTH], w[:, WIDTH:2 * WIDTH], w[:, 2 * WIDTH:3 * WIDTH], w[:, 3 * WIDTH:])
            gq = jnp.tile(q_gain_a[j], N_HEADS)[None, :]
            gk = jnp.tile(k_gain_a[j], N_HEADS)[None, :]
            w_out = w_out_a[j].astype(BF16)
            norms = (HEAD_DIM, HEAD_DIM, 0)
            q, k, kt, v, vt = _proj_in(y_p, g, [wq, wk, wv], [gq, gk], norms, ("n", "nt", "nt"), PROMPT_TM, seq)
            gate, = _proj_in(y_p, g, [wg], [], (0,), no_t[:1], PROMPT_TM, seq)
            os_, ls_ = [], []
            for gi in range(len(DILATED_CONFIGS)):
                o_g, l_g = _dil_prompt_group(q, k, v, tabs_a[gi], gi, batch, seq)
                os_.append(o_g)
                ls_.append(l_g)
            y_p = _proj_out(os_ + ls_, gate, y_p, w_out, PROMPT_TM)
            for gi, (window, _) in enumerate(DILATED_CONFIGS):
                keep = min(window, seq)
                fs = slice(gi * A_OUT, (gi + 1) * A_OUT)
                a_kp[gi].append(_heads_last(kt[:, fs, seq - keep:], batch, keep, H_GROUP, HEAD_DIM))
                a_vp[gi].append(_heads_last(vt[:, fs, seq - keep:], batch, keep, H_GROUP, HEAD_DIM))
            q, k, v = _proj_in(y_s, g, [wq, wk, wv], [gq, gk], norms, no_t[:3], n_s, n_s)
            gate, = _proj_in(y_s, g, [wg], [], (0,), no_t[:1], n_s, n_s)
            o = _dil_sample(q, k, v, caches_ak, caches_av, j, tps_a, tn_a, dec_batch)
            y_s = _proj_out([o], gate, y_s, w_out, n_s)
            a_ks.append(k.reshape(dec_batch, DEC_SEQ, N_HEADS, HEAD_DIM))
            a_vs.append(v.reshape(dec_batch, DEC_SEQ, N_HEADS, HEAD_DIM))
        else:
            w_in = (w_in_b if kind == 1 else w_in_c)[j].astype(BF16)
            w_out = (w_out_b if kind == 1 else w_out_c)[j].astype(BF16)
            ws = [w_in[:, c * WIDTH:(c + 1) * WIDTH] for c in range(4)]
            ws_p = [ws[0].T, ws[1].T, ws[2].T, ws[3]]
            or_p = ("t", "t", "t", "n")
            if kind == 1:
                lambda_init = float(0.8 - 0.6 * math.exp(-0.3 * i))
                gq = jnp.tile(q_gain_b[j], 2 * N_HEADS)[None, :]
                gk = jnp.tile(k_gain_b[j], 2 * N_HEADS)[None, :]
                groups = (DIFF_DIM, DIFF_DIM, 0, 0)
                gains_p, gains_s = [gq.T, gk.T], [gq, gk]
                lam_vecs = jnp.stack([lambda_q1[j], lambda_k1[j], lambda_q2[j], lambda_k2[j]])
                sub_gain = sub_gain_b[j][None, :]
            else:
                groups = (0, 0, 0, 0)
                gains_p, gains_s = [], []
            qt, kt, vt, gate = _proj_in(y_p, g, ws_p, gains_p, groups, or_p, PROMPT_TM, seq)
            qs, ks_, vs_, gate_s = _proj_in(y_s, g, ws, gains_s, groups, no_t, n_s, n_s)
            if kind == 1:
                o = _diff_prompt(qt, kt, vt, ext_b, lam_vecs, sub_gain.T, lambda_init, batch, seq)
                o_s = _diff_sample(qs, ks_, vs_, cache_b_k, cache_b_v, j, page_table, bias_sb, lam_vecs,
                                   sub_gain, lambda_init, dec_batch)
                b_kp.append(_heads_last(kt, batch, seq, 2 * N_HEADS, DIFF_DIM))
                b_vp.append(_heads_last(vt, batch, seq, N_HEADS, HEAD_DIM))
                b_ks.append(ks_.reshape(dec_batch, DEC_SEQ, 2 * N_HEADS, DIFF_DIM))
                b_vs.append(vs_.reshape(dec_batch, DEC_SEQ, N_HEADS, HEAD_DIM))
            else:
                o = _stick_prompt(qt, kt, vt, batch, seq)
                o_s = _stick_sample(qs, ks_, vs_, cache_c_k, cache_c_v, j, page_table, dec_batch)
                c_kp.append(_heads_last(kt, batch, seq, N_HEADS, HEAD_DIM))
                c_vp.append(_heads_last(vt, batch, seq, N_HEADS, HEAD_DIM))
                c_ks.append(ks_.reshape(dec_batch, DEC_SEQ, N_HEADS, HEAD_DIM))
                c_vs.append(vs_.reshape(dec_batch, DEC_SEQ, N_HEADS, HEAD_DIM))
            y_p = _proj_out([o], gate, y_p, w_out, PROMPT_TM)
            y_s = _proj_out([o_s], gate_s, y_s, w_out, n_s)

    return (y_p.reshape(batch, seq, D_MODEL), y_s.reshape(dec_batch, DEC_SEQ, D_MODEL),
            jnp.stack(a_kp[0]), jnp.stack(a_vp[0]), jnp.stack(a_kp[1]), jnp.stack(a_vp[1]),
            jnp.stack(a_kp[2]), jnp.stack(a_vp[2]), jnp.stack(a_ks), jnp.stack(a_vs),
            jnp.stack(b_kp), jnp.stack(b_vp), jnp.stack(b_ks), jnp.stack(b_vs),
            jnp.stack(c_kp), jnp.stack(c_vp), jnp.stack(c_ks), jnp.stack(c_vs))
```

```python
import functools
import math

import numpy as np
import jax
import jax.numpy as jnp
from jax import lax
from jax.experimental import pallas as pl
from jax.experimental.pallas import tpu as pltpu

F32 = jnp.float32
BF16 = jnp.bfloat16

D_MODEL = 1024
N_HEADS = 24
HEAD_DIM = 64
WIDTH = N_HEADS * HEAD_DIM
H_GROUP = 8
A_OUT = H_GROUP * HEAD_DIM
DIFF_DIM = HEAD_DIM // 2
DILATED_CONFIGS = ((128, 1), (512, 4), (2048, 16))
NUM_BUCKETS = 32
REL_MAX_DIST = 2048
EPS = 1e-6
PAGE_SIZE = 128
DEC_SEQ = 4
NEG = -1e30

LANES = 128
MXU_DIM = 256
VMEM_LIMIT = 52 * 1024 * 1024

BIAS_TABLE_LEN = 8320


def _cparams(sem):
    return pltpu.CompilerParams(dimension_semantics=sem, vmem_limit_bytes=VMEM_LIMIT)


def _nt_dot(a, b):
    return lax.dot_general(a, b, (((1,), (1,)), ((), ())), preferred_element_type=F32)


def _dot(a, b):
    return jnp.dot(a, b, preferred_element_type=F32)


def _split_bf16(x):
    hi = x.astype(BF16)
    lo = (x - hi.astype(F32)).astype(BF16)
    return hi, lo


def _positions_minor(cache):
    n, pool, pos, heads, dim = cache.shape
    return jnp.transpose(cache, (0, 1, 3, 4, 2)).reshape(n * pool, heads * dim, pos)


def _bucket_thresholds():
    max_exact = NUM_BUCKETS // 2
    n = np.arange(1, 1 << 14, dtype=np.float64)
    large = max_exact + (np.log(n / max_exact) / math.log(REL_MAX_DIST / max_exact)
                         * (NUM_BUCKETS - max_exact)).astype(np.int64)
    return tuple(int(n[large >= max_exact + k][0]) for k in range(1, NUM_BUCKETS - max_exact))


_BUCKET_THRESHOLDS = _bucket_thresholds()


def _bias_table_kernel(bt_ref, o_ref):
    max_exact = NUM_BUCKETS // 2
    n = lax.broadcasted_iota(jnp.int32, (N_HEADS, BIAS_TABLE_LEN), 1)
    large = jnp.full(n.shape, max_exact, jnp.int32)
    for thr in _BUCKET_THRESHOLDS:
        large = large + (n >= thr).astype(jnp.int32)
    bucket = jnp.where(n < max_exact, n, jnp.minimum(large, NUM_BUCKETS - 1))
    bt = bt_ref[...]
    acc = jnp.zeros(n.shape, F32)
    for b in range(NUM_BUCKETS):
        acc = jnp.where(bucket == b, jnp.broadcast_to(bt[:, b:b + 1], n.shape), acc)
    o_ref[...] = acc


def _bias_table(rel_bias):
    return pl.pallas_call(
        _bias_table_kernel,
        out_shape=jax.ShapeDtypeStruct((N_HEADS, BIAS_TABLE_LEN), F32),
        name="bias_table",
    )(rel_bias.T)


def _toeplitz(v, n_rows, n_cols):
    h = v.shape[0]
    p = n_rows + n_cols + 1
    pad = jnp.zeros((h, 1), v.dtype)
    w = jnp.concatenate([pad, v[:, ::-1], pad], axis=1)
    x = jnp.tile(w[:, None, :], (1, n_rows, 1)).reshape(h, n_rows * p)
    x = x[:, :n_rows * (p - 1)].reshape(h, n_rows, p - 1)
    return x[:, :, n_rows:n_rows + n_cols]


PROJ_TN = 512


def _group_sumsq(acc, bd, axis):
    hi, lo = _split_bf16(acc * acc)
    parts = []
    for c in range(PROJ_TN // MXU_DIM):
        sl = slice(c * MXU_DIM, (c + 1) * MXU_DIM)
        if axis == 1:
            parts.append(_dot(hi[:, sl], bd) + _dot(lo[:, sl], bd))
        else:
            parts.append(_dot(bd, hi[sl, :]) + _dot(bd, lo[sl, :]))
    return jnp.concatenate(parts, axis=axis)


def _proj_in_kernel(*refs, groups, orient):
    n_w = len(groups)
    n_g = sum(1 for gs in groups if gs)
    x_ref, g_ref = refs[0], refs[1]
    w_refs = refs[2:2 + n_w]
    gain_refs = list(refs[2 + n_w:2 + n_w + n_g])
    bd_ref = refs[2 + n_w + n_g]
    out_refs = list(refs[3 + n_w + n_g:-1])
    h_scr = refs[-1]

    @pl.when(pl.program_id(1) == 0)
    def _():
        x = x_ref[...]
        ms = jnp.mean(x * x, axis=-1, keepdims=True)
        h_scr[...] = (x * lax.rsqrt(ms + EPS) * g_ref[...]).astype(BF16)

    h = h_scr[...]
    for k in range(n_w):
        gs = groups[k]
        if orient[k] == "n1":
            out_ref = out_refs.pop(0)

            @pl.when(pl.program_id(1) == 0)
            def _(w_ref=w_refs[k], out_ref=out_ref):
                out_ref[...] = _dot(h_scr[...], w_ref[...])
            continue
        if orient[k] == "t":
            acc = _nt_dot(w_refs[k][...], h)
        else:
            acc = _dot(h, w_refs[k][...])
        if gs:
            bd = bd_ref[0 if gs == HEAD_DIM else 1]
            ss = _group_sumsq(acc, bd, 0 if orient[k] == "t" else 1)
            acc = acc * lax.rsqrt(ss * (1.0 / gs) + EPS) * gain_refs.pop(0)[...]
        if orient[k] == "t":
            out_refs.pop(0)[0] = acc
        else:
            out_refs.pop(0)[...] = acc
            if orient[k] == "nt":
                out_refs.pop(0)[0] = acc.T


def _block_diag_ones():
    r = np.arange(MXU_DIM)
    b64 = (r[:, None] // HEAD_DIM == r[None, :] // HEAD_DIM)
    b32 = (r[:, None] // DIFF_DIM == r[None, :] // DIFF_DIM)
    return jnp.asarray(np.stack([b64, b32]).astype(np.float32), dtype=BF16)


def _proj_in(x, g, weights, gains, groups, orient, tm, seq):
    m, d = x.shape
    tn = PROJ_TN
    n_w = len(weights)
    w_dim = weights[0].shape[0] if orient[0] == "t" else weights[0].shape[1]
    spb = seq // tm
    in_specs = [pl.BlockSpec((tm, d), lambda i, j: (i, 0)),
                pl.BlockSpec((1, d), lambda i, j: (0, 0))]
    for k in range(n_w):
        if orient[k] == "t":
            in_specs.append(pl.BlockSpec((tn, d), lambda i, j: (j, 0)))
        elif orient[k] == "n1":
            assert weights[k].shape[1] == tn and not groups[k]
            in_specs.append(pl.BlockSpec((d, tn), lambda i, j: (0, 0)))
        else:
            in_specs.append(pl.BlockSpec((d, tn), lambda i, j: (0, j)))
    for k in range(n_w):
        if groups[k]:
            if orient[k] == "t":
                in_specs.append(pl.BlockSpec((tn, 1), lambda i, j: (j, 0)))
            else:
                in_specs.append(pl.BlockSpec((1, tn), lambda i, j: (0, j)))
    in_specs.append(pl.BlockSpec((2, MXU_DIM, MXU_DIM), lambda i, j: (0, 0, 0)))
    out_specs, out_shape = [], []
    for k in range(n_w):
        if orient[k] == "n1":
            out_specs.append(pl.BlockSpec((tm, tn), lambda i, j: (i, 0)))
            out_shape.append(jax.ShapeDtypeStruct((m, tn), F32))
            continue
        if "n" in orient[k]:
            out_specs.append(pl.BlockSpec((tm, tn), lambda i, j: (i, j)))
            out_shape.append(jax.ShapeDtypeStruct((m, w_dim), F32))
        if "t" in orient[k]:
            out_specs.append(pl.BlockSpec((1, tn, tm), lambda i, j: (i // spb, j, i % spb)))
            out_shape.append(jax.ShapeDtypeStruct((m // seq, w_dim, seq), F32))
    return pl.pallas_call(
        functools.partial(_proj_in_kernel, groups=tuple(groups), orient=tuple(orient)),
        grid=(m // tm, w_dim // tn),
        in_specs=in_specs,
        out_specs=out_specs,
        out_shape=out_shape,
        scratch_shapes=[pltpu.VMEM((tm, d), BF16)],
        compiler_params=_cparams(("parallel", "arbitrary")),
        name="proj_in",
    )(x, g, *weights, *gains, _block_diag_ones())


def _proj_out_kernel(*refs, merge):
    if merge:
        o0, o1, o2, l0, l1, l2, gate_ref, x_ref, w_ref, out_ref = refs
        la, lb, lc = l0[...], l1[...], l2[...]
        m = jnp.maximum(jnp.maximum(la, lb), lc)
        ea, eb, ec = jnp.exp(la - m), jnp.exp(lb - m), jnp.exp(lc - m)
        o = (ea * o0[...] + eb * o1[...] + ec * o2[...]) / (ea + eb + ec)
    else:
        o_ref, gate_ref, x_ref, w_ref, out_ref = refs
        o = o_ref[...]
    gate = gate_ref[...]
    act = o * (gate / (1.0 + jnp.exp(-gate)))
    out_ref[...] = x_ref[...] + _dot(act.astype(BF16), w_ref[...])


def _proj_out(o_list, gate, x, w, tm):
    m, wd = gate.shape
    d = x.shape[1]
    merge = len(o_list) > 1
    row = lambda i: (i, 0)
    in_specs = [pl.BlockSpec((tm, wd), row) for _ in o_list]
    in_specs += [pl.BlockSpec((tm, wd), row), pl.BlockSpec((tm, d), row),
                 pl.BlockSpec((wd, d), lambda i: (0, 0))]
    return pl.pallas_call(
        functools.partial(_proj_out_kernel, merge=merge),
        grid=(m // tm,),
        in_specs=in_specs,
        out_specs=pl.BlockSpec((tm, d), row),
        out_shape=jax.ShapeDtypeStruct((m, d), F32),
        compiler_params=_cparams(("parallel",)),
        name="proj_out",
    )(*o_list, gate, x, w)


A_BLK = 128


A_HEADS = LANES // HEAD_DIM
A_UNROLL = 4


def _tn_dot(a, b):
    return lax.dot_general(a, b, (((0,), (0,)), ((), ())), preferred_element_type=F32)


def _dil_prompt_kernel(*refs, dil, use_prev):
    if use_prev:
        q_ref, kc_ref, vc_ref, kp_ref, vp_ref, t_ref, o_ref, l_ref = refs
    else:
        q_ref, kc_ref, vc_ref, t_ref, o_ref, l_ref = refs
    scale = HEAD_DIM ** -0.5
    first = jnp.minimum(pl.program_id(2), 1)

    def residues(chains):
        heads = [(i, h) for i in range(len(chains)) for h in range(A_HEADS)]
        rows = [c[0] for c in chains]
        tab = [c[2] for c in chains]
        q2 = [(q_ref[0, rw, :] * scale).astype(BF16) for rw in rows]
        kc2 = [kc_ref[0, rw, :].astype(BF16) for rw in rows]
        sl = [slice(HEAD_DIM * h, HEAD_DIM * (h + 1)) for h in range(A_HEADS)]
        st = [_nt_dot(kc2[i][:, sl[h]], q2[i][:, sl[h]]) + t_ref[tab[i], h, A_BLK:, :] for i, h in heads]
        m = [jnp.max(s, axis=0, keepdims=True) for s in st]
        if use_prev:
            kp2 = [(kp_ref if c[1][0] else kc_ref)[0, c[1][1], :].astype(BF16) for c in chains]
            sp = [_nt_dot(kp2[i][:, sl[h]], q2[i][:, sl[h]]) + t_ref[tab[i], h, :A_BLK, :] for i, h in heads]
            m = [jnp.maximum(a, jnp.max(s, axis=0, keepdims=True)) for a, s in zip(m, sp)]
        e = [jnp.exp(s - a) for s, a in zip(st, m)]
        l = [jnp.sum(x, axis=0, keepdims=True) for x in e]
        vc2 = [vc_ref[0, rw, :].astype(BF16) for rw in rows]
        o = [_tn_dot(vc2[i][:, sl[h]], e[n].astype(BF16)) for n, (i, h) in enumerate(heads)]
        if use_prev:
            ep = [jnp.exp(s - a) for s, a in zip(sp, m)]
            l = [a + jnp.sum(x, axis=0, keepdims=True) for a, x in zip(l, ep)]
            vp2 = [(vp_ref if c[1][0] else vc_ref)[0, c[1][1], :].astype(BF16) for c in chains]
            o = [a + _tn_dot(vp2[i][:, sl[h]], ep[n].astype(BF16)) for n, ((i, h), a) in enumerate(zip(heads, o))]
        o = [a / b for a, b in zip(o, l)]
        lse = [jnp.broadcast_to(a + jnp.log(b), (HEAD_DIM, A_BLK)) for a, b in zip(m, l)]
        for i, rw in enumerate(rows):
            o_ref[0, rw, :] = jnp.concatenate(o[A_HEADS * i:A_HEADS * (i + 1)], axis=0).T
            l_ref[0, rw, :] = jnp.concatenate(lse[A_HEADS * i:A_HEADS * (i + 1)], axis=0).T

    def strided(r):
        rw = pl.ds(r, A_BLK, stride=dil)
        return (rw, (True, rw), first)

    if dil == 1:
        chains = [(pl.ds(0, A_BLK), (True, pl.ds(0, A_BLK)), first)]
        chains += [(pl.ds(A_BLK * j, A_BLK), (False, pl.ds(A_BLK * (j - 1), A_BLK)), 1) for j in range(1, A_UNROLL)]
        residues(chains)
    elif dil <= A_UNROLL:
        residues([strided(r) for r in range(dil)])
    else:
        def body(i, carry):
            residues([strided(i * A_UNROLL + u) for u in range(A_UNROLL)])
            return carry
        lax.fori_loop(0, dil // A_UNROLL, body, 0)


def _group_bias(bias_tab, g):
    window, dil = DILATED_CONFIGS[g]
    n_keys = window // dil + 1
    return bias_tab[g * H_GROUP:(g + 1) * H_GROUP, 0:n_keys * dil:dil]


def _dil_prompt_tables(bias_tab):
    tables = []
    for g in range(len(DILATED_CONFIGS)):
        bj = _group_bias(bias_tab, g)
        lo, hi = A_BLK - (2 * A_BLK - 1), A_BLK + A_BLK - 1
        v = jnp.full((H_GROUP, hi - lo + 1), NEG, F32)
        v = lax.dynamic_update_slice(v, bj, (0, -lo))
        t_gen = _toeplitz(v, A_BLK, 2 * A_BLK)
        t_first = t_gen.at[:, :, :A_BLK].set(NEG)
        tables.append(jnp.swapaxes(jnp.stack([t_first, t_gen]), 2, 3))
    return tables


def _dil_prompt_group(q, k, v, table, g, batch, seq):
    dil = DILATED_CONFIGS[g][1]
    rows = A_BLK * (dil if dil > 1 else A_UNROLL)
    prev_rows = rows if dil > 1 else A_BLK
    nqb = seq // rows
    use_prev = nqb > 1
    n_hp = A_OUT // LANES
    q3, k3, v3 = (a.reshape(batch, seq, WIDTH) for a in (q, k, v))
    cur = lambda b, h, i: (b, i, g * n_hp + h)
    prev = lambda b, h, i: (b, jnp.maximum(i * (rows // prev_rows) - 1, 0), g * n_hp + h)
    blk = (1, rows, LANES)
    in_specs = [pl.BlockSpec(blk, cur), pl.BlockSpec(blk, cur), pl.BlockSpec(blk, cur)]
    args = [q3, k3, v3]
    if use_prev:
        in_specs += [pl.BlockSpec((1, prev_rows, LANES), prev)] * 2
        args += [k3, v3]
    in_specs += [pl.BlockSpec((2, A_HEADS, 2 * A_BLK, A_BLK), lambda b, h, i: (0, h, 0, 0))]
    args += [table]
    out_map = lambda b, h, i: (b, i, h)
    o, lse = pl.pallas_call(
        functools.partial(_dil_prompt_kernel, dil=dil, use_prev=use_prev),
        grid=(batch, n_hp, nqb),
        in_specs=in_specs,
        out_specs=[pl.BlockSpec(blk, out_map), pl.BlockSpec(blk, out_map)],
        out_shape=[jax.ShapeDtypeStruct((batch, seq, A_OUT), F32)] * 2,
        compiler_params=_cparams(("parallel", "parallel", "arbitrary")),
        name="dilated_prompt_g%d" % g,
    )(*args)
    return o.reshape(batch * seq, A_OUT), lse.reshape(batch * seq, A_OUT)


NEW_PAD = 16
A_ROWS = H_GROUP * DEC_SEQ


def _dil_sample_kernel(qbd_ref, kn_ref, vn_ref, k0, v0, k1, v1, k2, v2, tp0, tp1, tp2, tn_ref, o_ref):
    scale = HEAD_DIM ** -0.5
    outs, lses = [], []
    for g, (kr, vr, tp) in enumerate(((k0, v0, tp0), (k1, v1, tp1), (k2, v2, tp2))):
        gs = slice(A_OUT * g, A_OUT * (g + 1))
        qb = qbd_ref[0, g]
        s = _dot(qb, kr[0].astype(BF16)) * scale + tp[...]
        sn = _nt_dot(qb, kn_ref[0, :, gs].astype(BF16)) * scale + tn_ref[g]
        m = jnp.maximum(jnp.max(s, axis=1, keepdims=True), jnp.max(sn, axis=1, keepdims=True))
        e = jnp.exp(s - m)
        en = jnp.exp(sn - m)
        l = jnp.sum(e, axis=1, keepdims=True) + jnp.sum(en, axis=1, keepdims=True)
        o = _nt_dot(e.astype(BF16), vr[0].astype(BF16)) + _dot(en.astype(BF16), vn_ref[0, :, gs].astype(BF16))
        outs.append(o / l)
        lses.append(m + jnp.log(l))
    mm = jnp.maximum(jnp.maximum(lses[0], lses[1]), lses[2])
    es = [jnp.exp(x - mm) for x in lses]
    o_ref[0] = (es[0] * outs[0] + es[1] * outs[1] + es[2] * outs[2]) / (es[0] + es[1] + es[2])


def _dil_sample_tables(bias_tab):
    tps, tns = [], []
    tq = np.arange(DEC_SEQ)
    for g, (window, dil) in enumerate(DILATED_CONFIGS):
        bg = bias_tab[g * H_GROUP:(g + 1) * H_GROUP]
        length = window
        pos = np.arange(length)
        dist = length + tq[:, None] - pos[None, :]
        ok = (dist % dil == 0) & (dist <= window)
        rows = jnp.stack([bg[:, t + 1:t + 1 + length][:, ::-1] for t in range(DEC_SEQ)], axis=1)
        tps.append(jnp.where(jnp.asarray(ok)[None], rows, NEG).reshape(A_ROWS, length))
        tn_ = np.arange(NEW_PAD)
        dn = tq[:, None] - tn_[None, :]
        okn = (dn >= 0) & (tn_[None, :] < DEC_SEQ) & (dn % dil == 0)
        vals = bg[:, np.clip(dn, 0, DEC_SEQ)]
        tns.append(jnp.where(jnp.asarray(okn)[None], vals, NEG).reshape(A_ROWS, NEW_PAD))
    return tps, jnp.stack(tns)


def _pad_new(x, dec_batch, rows):
    x = x.reshape(dec_batch, DEC_SEQ, x.shape[-1])
    return jnp.pad(x, ((0, 0), (0, rows - DEC_SEQ), (0, 0)))


def _dil_sample(q, k, v, caches_k, caches_v, j, tps, tn, dec_batch):
    n_grp = len(DILATED_CONFIGS)
    sel = np.zeros((H_GROUP, DEC_SEQ, A_ROWS), np.float32)
    for h in range(H_GROUP):
        for t in range(DEC_SEQ):
            sel[h, t, h * DEC_SEQ + t] = 1.0
    q5 = q.reshape(dec_batch, DEC_SEQ, n_grp, H_GROUP, HEAD_DIM)
    qbd = jnp.einsum("btghd,htr->bgrhd", q5, jnp.asarray(sel), precision=lax.Precision.HIGHEST)
    qbd = qbd.reshape(dec_batch, n_grp, A_ROWS, A_OUT).astype(BF16)
    kn, vn = (_pad_new(a, dec_batch, NEW_PAD) for a in (k, v))
    args = [qbd, kn, vn]
    in_specs = [pl.BlockSpec((1, n_grp, A_ROWS, A_OUT), lambda b: (b, 0, 0, 0)),
                pl.BlockSpec((1, NEW_PAD, WIDTH), lambda b: (b, 0, 0)),
                pl.BlockSpec((1, NEW_PAD, WIDTH), lambda b: (b, 0, 0))]
    for g, (window, dil) in enumerate(DILATED_CONFIGS):
        for c in (caches_k[g], caches_v[g]):
            args.append(_positions_minor(c))
            in_specs.append(pl.BlockSpec((1, A_OUT, window), lambda b, j=j: (j * dec_batch + b, 0, 0)))
    for tp in tps:
        args.append(tp)
        in_specs.append(pl.BlockSpec(tp.shape, lambda b: (0, 0)))
    args.append(tn)
    in_specs.append(pl.BlockSpec(tn.shape, lambda b: (0, 0, 0)))
    o = pl.pallas_call(
        _dil_sample_kernel,
        grid=(dec_batch,),
        in_specs=in_specs,
        out_specs=pl.BlockSpec((1, A_ROWS, A_OUT), lambda b: (b, 0, 0)),
        out_shape=jax.ShapeDtypeStruct((dec_batch, A_ROWS, A_OUT), F32),
        compiler_params=_cparams(("parallel",)),
        name="dilated_sample",
    )(*args)
    o = o.reshape(dec_batch, H_GROUP, DEC_SEQ, H_GROUP, HEAD_DIM)
    idx = np.arange(H_GROUP)
    o = o[:, idx, :, idx, :]
    return jnp.transpose(o, (1, 2, 0, 3)).reshape(dec_batch * DEC_SEQ, A_OUT)


BC_BLK = 256
HEADS_PER_STEP = MXU_DIM // HEAD_DIM


def _lambda_value(lam_ref, lambda_init):
    lv = lam_ref[...]
    s1 = jnp.sum(lv[0:1] * lv[1:2], axis=1, keepdims=True)
    s2 = jnp.sum(lv[2:3] * lv[3:4], axis=1, keepdims=True)
    return jnp.exp(s1) - jnp.exp(s2) + lambda_init


LOG2E = 1.4426950408889634


def _diff_prompt_kernel(qt_ref, k_ref, vt_ref, ext_ref, lam_ref, sg_ref, o_ref, tab_scr, st_scr, p_scr, *,
                        lambda_init, nqb):
    b = pl.program_id(1)
    qi = pl.program_id(2)
    scale2 = DIFF_DIM ** -0.5 * LOG2E

    @pl.when((b == 0) & (qi == 0))
    def _():
        for hl in range(HEADS_PER_STEP):
            for dd in range(nqb):
                row = ext_ref[0, hl:hl + 1, dd * BC_BLK:(dd + 2) * BC_BLK] * LOG2E
                x = jnp.broadcast_to(row, (BC_BLK, 2 * BC_BLK))
                tab_scr[hl, dd] = pltpu.roll(x, 0, 1, stride=1, stride_axis=0)[:, BC_BLK:]

    streams = [(hl, half) for hl in range(HEADS_PER_STEP) for half in range(2)]
    qts = []
    for hl, half in streams:
        c0 = HEAD_DIM * hl + DIFF_DIM * half
        qts.append((qt_ref[0, c0:c0 + DIFF_DIM, :] * scale2).astype(BF16))

    def body(kb, carry):
        return update(carry, [kb])

    def update(carry, kbs):
        ms, ls, accs = carry
        keys = [pl.ds(pl.multiple_of(kb * BC_BLK, BC_BLK), BC_BLK) for kb in kbs]
        ms_n, ls_n, accs_n, alphas = [], [], [], []
        for s, (hl, half) in enumerate(streams):
            c0 = HEAD_DIM * hl + DIFF_DIM * half
            m = ms[s]
            for j, kb in enumerate(kbs):
                ks = k_ref[0, c0:c0 + DIFF_DIM, keys[j]].astype(BF16)
                st = _tn_dot(ks, qts[s]) + tab_scr[hl, qi - kb]
                st_scr[j, s] = st
                m = jnp.maximum(m, jnp.max(st, axis=0, keepdims=True))
            ms_n.append(m)
        for s in range(len(streams)):
            alphas.append(jnp.exp2(ms[s] - ms_n[s]))
            l = alphas[s] * ls[s]
            for j in range(len(kbs)):
                p = jnp.exp2(st_scr[j, s] - ms_n[s])
                l = l + jnp.sum(p, axis=0, keepdims=True)
                p_scr[j, s] = p.astype(BF16)
            ls_n.append(l)
        for s, (hl, half) in enumerate(streams):
            acc = alphas[s] * accs[s]
            for j in range(len(kbs)):
                vt = vt_ref[0, HEAD_DIM * hl:HEAD_DIM * (hl + 1), keys[j]].astype(BF16)
                acc = acc + _dot(vt, p_scr[j, s])
            accs_n.append(acc)
        return tuple(ms_n), tuple(ls_n), tuple(accs_n)

    n_s = len(streams)
    init = (tuple(jnp.full((1, BC_BLK), NEG, F32) for _ in range(n_s)),
            tuple(jnp.zeros((1, BC_BLK), F32) for _ in range(n_s)),
            tuple(jnp.zeros((HEAD_DIM, BC_BLK), F32) for _ in range(n_s)))
    n_kb = qi + 1
    carry = lax.fori_loop(0, n_kb // 2, lambda i, c: update(c, [2 * i, 2 * i + 1]), init)
    _, ls, accs = lax.cond(n_kb % 2 == 1, lambda c: body(qi, c), lambda c: c, carry)

    lam = _lambda_value(lam_ref, lambda_init)
    outs = []
    for hl in range(HEADS_PER_STEP):
        o = accs[2 * hl] / ls[2 * hl] - lam * (accs[2 * hl + 1] / ls[2 * hl + 1])
        ms = jnp.mean(o * o, axis=0, keepdims=True)
        outs.append(o * lax.rsqrt(ms + EPS) * sg_ref[...] * (1.0 - lambda_init))
    o_ref[0] = jnp.concatenate(outs, axis=0).T


def _diff_prompt_bias_rows(bias_tab, seq):
    ext = jnp.concatenate([jnp.full((N_HEADS, BC_BLK), NEG, F32), bias_tab[:, :seq + BC_BLK]], axis=1)
    return ext.reshape(N_HEADS // HEADS_PER_STEP, HEADS_PER_STEP, seq + 2 * BC_BLK)


def _diff_prompt(qt, kt, vt, ext, lam_vecs, sub_gain_col, lambda_init, batch, seq):
    nqb = seq // BC_BLK
    n_hg = WIDTH // MXU_DIM
    o = pl.pallas_call(
        functools.partial(_diff_prompt_kernel, lambda_init=lambda_init, nqb=nqb),
        grid=(n_hg, batch, nqb),
        in_specs=[pl.BlockSpec((1, MXU_DIM, BC_BLK), lambda h, b, i: (b, h, i)),
                  pl.BlockSpec((1, MXU_DIM, seq), lambda h, b, i: (b, h, 0)),
                  pl.BlockSpec((1, MXU_DIM, seq), lambda h, b, i: (b, h, 0)),
                  pl.BlockSpec((1, HEADS_PER_STEP, seq + 2 * BC_BLK), lambda h, b, i: (h, 0, 0)),
                  pl.BlockSpec((4, DIFF_DIM), lambda h, b, i: (0, 0)),
                  pl.BlockSpec((HEAD_DIM, 1), lambda h, b, i: (0, 0))],
        out_specs=pl.BlockSpec((1, BC_BLK, MXU_DIM), lambda h, b, i: (b, i, h)),
        out_shape=jax.ShapeDtypeStruct((batch, seq, WIDTH), F32),
        scratch_shapes=[pltpu.VMEM((HEADS_PER_STEP, nqb, BC_BLK, BC_BLK), F32),
                        pltpu.VMEM((2, 2 * HEADS_PER_STEP, BC_BLK, BC_BLK), F32),
                        pltpu.VMEM((2, 2 * HEADS_PER_STEP, BC_BLK, BC_BLK), BF16)],
        compiler_params=_cparams(("arbitrary", "arbitrary", "arbitrary")),
        name="diff_prompt",
    )(qt, kt, vt, ext, lam_vecs, sub_gain_col)
    return o.reshape(batch * seq, WIDTH)


def _log_sigmoid(z):
    return jnp.minimum(z, 0.0) - jnp.log(1.0 + jnp.exp(-jnp.abs(z)))


def _suffix_sum_rows(x, tri):
    hi, lo = _split_bf16(x)
    return _dot(hi, tri) + _dot(lo, tri)


STICK_HEADS = MXU_DIM // HEAD_DIM
UNDERFLOW = 104.0


def _stick_prompt_kernel(qt_ref, k_ref, vt_ref, tri_ref, o_ref, lsc_scr, hilo_scr, a_scr):
    qi = pl.program_id(2)
    scale = HEAD_DIM ** -0.5
    tri = tri_ref[...]
    key_i = lax.broadcasted_iota(jnp.int32, (BC_BLK, BC_BLK), 0)
    qry_i = lax.broadcasted_iota(jnp.int32, (BC_BLK, BC_BLK), 1)
    strict = key_i < qry_i
    qts = [(qt_ref[0, HEAD_DIM * h:HEAD_DIM * (h + 1), :] * scale).astype(BF16) for h in range(STICK_HEADS)]

    def block(kb, cs, accs, masked):
        keys = pl.ds(pl.multiple_of(kb * BC_BLK, BC_BLK), BC_BLK)
        cs_n, accs_n, edge = [], [], []

        def stage_a(h):
            fs = slice(HEAD_DIM * h, HEAD_DIM * (h + 1))
            z = _tn_dot(k_ref[0, fs, keys].astype(BF16), qts[h])
            neg_abs = lax.bitcast_convert_type(
                lax.bitcast_convert_type(z, jnp.uint32) | jnp.uint32(0x80000000), F32)
            t = jnp.log(1.0 + jnp.exp(neg_abs))
            ls = jnp.minimum(z, 0.0) - t
            stay = ls - z
            if masked:
                stay = jnp.where(strict, stay, 0.0)
            hi, lo = _split_bf16(stay)
            hilo_scr[h, :BC_BLK] = hi
            hilo_scr[h, BC_BLK:] = lo
            lsc_scr[h] = ls + cs[h]
            edge.append(stay[0:1] - ls[0:1])

        def stage_b(h):
            x = lsc_scr[h] + _dot(tri, hilo_scr[h])
            a = jnp.exp(x)
            if masked:
                a = jnp.where(strict, a, 0.0)
            a_scr[h] = a.astype(BF16)
            cs_n.append(x[0:1] + edge[h])

        def stage_c(h):
            fs = slice(HEAD_DIM * h, HEAD_DIM * (h + 1))
            accs_n.append(accs[h] + _dot(vt_ref[0, fs, keys].astype(BF16), a_scr[h]))

        for stage in (stage_a, stage_b, stage_c):
            for h in range(STICK_HEADS):
                stage(h)
        return tuple(cs_n), tuple(accs_n)

    def alive(cs):
        top = cs[0]
        for c in cs[1:]:
            top = jnp.maximum(top, c)
        return jnp.max(top) > -UNDERFLOW

    cs = tuple(jnp.zeros((1, BC_BLK), F32) for _ in range(STICK_HEADS))
    accs = tuple(jnp.zeros((HEAD_DIM, BC_BLK), F32) for _ in range(STICK_HEADS))
    cs, accs = block(qi, cs, accs, True)

    def cond(carry):
        kb, go, _, _ = carry
        return (kb >= 0) & go

    def body(carry):
        kb, _, cs, accs = carry
        cs, accs = block(kb, cs, accs, False)
        return kb - 1, alive(cs), cs, accs

    _, _, _, accs = lax.while_loop(cond, body, (qi - 1, alive(cs), cs, accs))
    o_ref[0] = jnp.concatenate(accs, axis=0).T


def _tri_later(n):
    r = np.arange(n)
    return jnp.asarray((r[:, None] > r[None, :]).astype(np.float32), dtype=BF16)


def _stick_prompt(qt, kt, vt, batch, seq):
    nqb = seq // BC_BLK
    n_hg = WIDTH // MXU_DIM
    o = pl.pallas_call(
        _stick_prompt_kernel,
        grid=(n_hg, batch, nqb),
        in_specs=[pl.BlockSpec((1, MXU_DIM, BC_BLK), lambda h, b, i: (b, h, i)),
                  pl.BlockSpec((1, MXU_DIM, seq), lambda h, b, i: (b, h, 0)),
                  pl.BlockSpec((1, MXU_DIM, seq), lambda h, b, i: (b, h, 0)),
                  pl.BlockSpec((BC_BLK, 2 * BC_BLK), lambda h, b, i: (0, 0))],
        out_specs=pl.BlockSpec((1, BC_BLK, MXU_DIM), lambda h, b, i: (b, i, h)),
        out_shape=jax.ShapeDtypeStruct((batch, seq, WIDTH), F32),
        scratch_shapes=[pltpu.VMEM((STICK_HEADS, BC_BLK, BC_BLK), F32),
                        pltpu.VMEM((STICK_HEADS, 2 * BC_BLK, BC_BLK), BF16),
                        pltpu.VMEM((STICK_HEADS, BC_BLK, BC_BLK), BF16)],
        compiler_params=_cparams(("parallel", "parallel", "arbitrary")),
        name="stick_prompt",
    )(qt, kt, vt, jnp.tile(_tri_later(BC_BLK).T, (1, 2)))
    return o.reshape(batch * seq, WIDTH)


HEADS_PER_GROUP = MXU_DIM // HEAD_DIM
N_VGROUPS = WIDTH // MXU_DIM
B_ROWS = 2 * HEADS_PER_GROUP * DEC_SEQ
C_ROWS = HEADS_PER_GROUP * DEC_SEQ
B_QROWS = N_VGROUPS * B_ROWS
C_QROWS = N_VGROUPS * C_ROWS
PAGES_PER_STEP = 8


def _page_scores(q_ref, k_page_ref):
    parts = []
    for c in range(N_VGROUPS):
        kt = k_page_ref[0, MXU_DIM * c:MXU_DIM * (c + 1), :].astype(BF16)
        parts.append(_dot(q_ref[0, c], kt))
    return jnp.concatenate(parts, axis=0)


def _page_values(w, v_page_ref, rows):
    outs = []
    for c in range(N_VGROUPS):
        vt = v_page_ref[0, MXU_DIM * c:MXU_DIM * (c + 1), :].astype(BF16)
        outs.append(_nt_dot(w[rows * c:rows * (c + 1), :].astype(BF16), vt))
    return outs


def _diff_sample_kernel(pt_ref, q_ref, *refs, lambda_init, n_steps):
    k_refs = refs[:PAGES_PER_STEP]
    v_refs = refs[PAGES_PER_STEP:2 * PAGES_PER_STEP]
    (kn_ref, vn_ref, bias_ref, biasn_ref, lam_ref, sg_ref, o_ref,
     s_scr, m_scr, l_scr, acc_scr) = refs[2 * PAGES_PER_STEP:]
    ph = pl.program_id(1)
    p = pl.program_id(2)
    scale = DIFF_DIM ** -0.5
    n_pages = n_steps * PAGES_PER_STEP

    def score_pages(pages):
        m = m_scr[...]
        for slot, k_ref, bias in pages:
            s = _page_scores(q_ref, k_ref) * scale + bias
            s_scr[slot] = s
            m = jnp.maximum(m, s)
        m_scr[...] = m

    def value_pages(pages):
        l = l_scr[...]
        accs = [acc_scr[c] for c in range(N_VGROUPS)]
        for slot, v_ref in pages:
            e = jnp.exp(s_scr[slot] - m_scr[...])
            l = l + e
            for c, d in enumerate(_page_values(e, v_ref, B_ROWS)):
                accs[c] = accs[c] + d
        l_scr[...] = l
        for c in range(N_VGROUPS):
            acc_scr[c] = accs[c]

    @pl.when((ph == 0) & (p == 0))
    def _():
        m_scr[...] = jnp.full(m_scr.shape, NEG, F32)
        l_scr[...] = jnp.zeros(l_scr.shape, F32)
        acc_scr[...] = jnp.zeros(acc_scr.shape, F32)

    @pl.when(ph == 0)
    def _():
        score_pages([(p * PAGES_PER_STEP + g, k_refs[g], bias_ref[:, PAGE_SIZE * g:PAGE_SIZE * (g + 1)])
                     for g in range(PAGES_PER_STEP)])

    @pl.when((ph == 0) & (p == n_steps - 1))
    def _():
        score_pages([(n_pages, kn_ref, biasn_ref[...])])

    @pl.when((ph == 1) & (p == 0))
    def _():
        m_scr[...] = jnp.broadcast_to(jnp.max(m_scr[...], axis=1, keepdims=True), m_scr.shape)

    @pl.when(ph == 1)
    def _():
        value_pages([(p * PAGES_PER_STEP + g, v_refs[g]) for g in range(PAGES_PER_STEP)])

    @pl.when((ph == 1) & (p == n_steps - 1))
    def _():
        value_pages([(n_pages, vn_ref)])
        lam = _lambda_value(lam_ref, lambda_init)
        l = jnp.sum(l_scr[...], axis=1, keepdims=True)
        half_rows = B_ROWS // 2
        r = lax.broadcasted_iota(jnp.int32, (half_rows, MXU_DIM), 0)
        ln = lax.broadcasted_iota(jnp.int32, (half_rows, MXU_DIM), 1)
        own = (r // DEC_SEQ) == (ln // HEAD_DIM)
        for c in range(N_VGROUPS):
            an = acc_scr[c] / l[B_ROWS * c:B_ROWS * (c + 1)]
            d = jnp.where(own, an[:half_rows] - lam * an[half_rows:], 0.0)
            ms = jnp.sum(d * d, axis=1, keepdims=True) * (1.0 / HEAD_DIM)
            o_ref[0, c] = d * lax.rsqrt(ms + EPS) * sg_ref[...] * (1.0 - lambda_init)


def _diag_blocks(x, dec_batch):
    x = x.reshape(dec_batch, N_VGROUPS, HEADS_PER_GROUP, DEC_SEQ, HEADS_PER_GROUP, HEAD_DIM)
    idx = np.arange(HEADS_PER_GROUP)
    x = x[:, :, idx, :, idx, :]
    x = jnp.transpose(x, (1, 3, 2, 0, 4))
    return x.reshape(dec_batch * DEC_SEQ, WIDTH)


def _query_tiles(q, dec_batch, sub_dim, rows, row_of):
    n_sub = MXU_DIM // sub_dim
    sel = np.zeros((n_sub, DEC_SEQ, rows), np.float32)
    for j in range(n_sub):
        for t in range(DEC_SEQ):
            sel[j, t, row_of(j, t)] = 1.0
    q5 = q.reshape(dec_batch, DEC_SEQ, N_VGROUPS, n_sub, sub_dim)
    qt = jnp.einsum("btcjd,jtr->bcrjd", q5, jnp.asarray(sel), precision=lax.Precision.HIGHEST)
    return qt.reshape(dec_batch, N_VGROUPS, rows, MXU_DIM).astype(BF16)


def _b_row(j, t):
    return (j % 2) * (B_ROWS // 2) + (j // 2) * DEC_SEQ + t


def _c_row(hl, t):
    return hl * DEC_SEQ + t


def _diff_sample_bias(bias_tab, past_len):
    segs = [bias_tab[:, t + 1:t + 1 + past_len][:, ::-1] for t in range(DEC_SEQ)]
    past = jnp.stack(segs, axis=1)
    tn_ = np.arange(PAGE_SIZE)[None, :]
    tq = np.arange(DEC_SEQ)[:, None]
    dist = tq - tn_
    ok = (dist >= 0) & (tn_ < DEC_SEQ)
    new = jnp.where(jnp.asarray(ok)[None], bias_tab[:, np.clip(dist, 0, DEC_SEQ)], NEG)
    full = jnp.concatenate([past, new], axis=2)
    n_cols = past_len + PAGE_SIZE
    full = full.reshape(N_VGROUPS, 1, HEADS_PER_GROUP, DEC_SEQ, n_cols)
    full = jnp.broadcast_to(full, (N_VGROUPS, 2, HEADS_PER_GROUP, DEC_SEQ, n_cols))
    return full.reshape(B_QROWS, n_cols)


def _new_page(x, dec_batch):
    x = jnp.transpose(x.reshape(dec_batch, DEC_SEQ, WIDTH), (0, 2, 1))
    return jnp.pad(x, ((0, 0), (0, 0), (0, PAGE_SIZE - DEC_SEQ)))


def _diff_sample(q, k, v, cache_k, cache_v, j, page_table, bias_sb, lam_vecs, sub_gain, lambda_init, dec_batch):
    n_pool = cache_k.shape[1]
    n_pages = page_table.shape[1]
    n_steps = n_pages // PAGES_PER_STEP
    ck = _positions_minor(cache_k)
    cv = _positions_minor(cache_v)
    qt = _query_tiles(q, dec_batch, DIFF_DIM, B_ROWS, _b_row)
    kn = _new_page(k, dec_batch)
    vn = _new_page(v, dec_batch)
    sg = jnp.tile(sub_gain, (1, HEADS_PER_GROUP))
    last = n_steps - 1
    page = (1, WIDTH, PAGE_SIZE)

    def k_map(g):
        return lambda b, ph, p, pt: (j * n_pool + pt[b, jnp.where(ph == 0, p, last) * PAGES_PER_STEP + g], 0, 0)

    def v_map(g):
        return lambda b, ph, p, pt: (j * n_pool + pt[b, jnp.where(ph == 1, p, 0) * PAGES_PER_STEP + g], 0, 0)

    per_b = lambda b, ph, p, pt: (b, 0, 0)
    in_specs = [pl.BlockSpec((1, N_VGROUPS, B_ROWS, MXU_DIM), lambda b, ph, p, pt: (b, 0, 0, 0))]
    in_specs += [pl.BlockSpec(page, k_map(g)) for g in range(PAGES_PER_STEP)]
    in_specs += [pl.BlockSpec(page, v_map(g)) for g in range(PAGES_PER_STEP)]
    in_specs += [pl.BlockSpec(page, per_b), pl.BlockSpec(page, per_b),
                 pl.BlockSpec((B_QROWS, PAGES_PER_STEP * PAGE_SIZE),
                              lambda b, ph, p, pt: (0, jnp.where(ph == 0, p, last))),
                 pl.BlockSpec((B_QROWS, PAGE_SIZE), lambda b, ph, p, pt: (0, n_pages)),
                 pl.BlockSpec((4, DIFF_DIM), lambda b, ph, p, pt: (0, 0)),
                 pl.BlockSpec((1, MXU_DIM), lambda b, ph, p, pt: (0, 0))]
    grid_spec = pltpu.PrefetchScalarGridSpec(
        num_scalar_prefetch=1,
        grid=(dec_batch, 2, n_steps),
        in_specs=in_specs,
        out_specs=pl.BlockSpec((1, N_VGROUPS, B_ROWS // 2, MXU_DIM), lambda b, ph, p, pt: (b, 0, 0, 0)),
        scratch_shapes=[pltpu.VMEM((n_pages + 1, B_QROWS, PAGE_SIZE), F32),
                        pltpu.VMEM((B_QROWS, PAGE_SIZE), F32),
                        pltpu.VMEM((B_QROWS, PAGE_SIZE), F32),
                        pltpu.VMEM((N_VGROUPS, B_ROWS, MXU_DIM), F32)])
    o = pl.pallas_call(
        functools.partial(_diff_sample_kernel, lambda_init=lambda_init, n_steps=n_steps),
        grid_spec=grid_spec,
        out_shape=jax.ShapeDtypeStruct((dec_batch, N_VGROUPS, B_ROWS // 2, MXU_DIM), F32),
        compiler_params=_cparams(("parallel", "arbitrary", "arbitrary")),
        name="diff_sample",
    )(page_table, qt, *([ck] * PAGES_PER_STEP), *([cv] * PAGES_PER_STEP), kn, vn, bias_sb, bias_sb, lam_vecs, sg)
    return _diag_blocks(o, dec_batch)


def _stick_sample_kernel(pt_ref, q_ref, *refs, n_steps):
    k_refs = refs[:PAGES_PER_STEP]
    v_refs = refs[PAGES_PER_STEP:2 * PAGES_PER_STEP]
    kn_ref, vn_ref, valid_ref, tri_ref, o_ref, c_scr, acc_scr = refs[2 * PAGES_PER_STEP:]
    p = pl.program_id(1)
    scale = HEAD_DIM ** -0.5

    def pages(items):
        c = c_scr[...]
        accs = [acc_scr[g] for g in range(N_VGROUPS)]
        parts = []
        for k_ref, v_ref, valid in items:
            z = _page_scores(q_ref, k_ref) * scale
            ls = _log_sigmoid(z)
            stay = ls - z
            if valid is not None:
                stay = stay * valid
            parts.append((ls, stay, jnp.sum(stay, axis=1, keepdims=True)))
        for (k_ref, v_ref, valid), (ls, stay, total) in zip(items, parts):
            after = _suffix_sum_rows(stay, tri_ref[...])
            a = jnp.exp(ls + after + c)
            if valid is not None:
                a = a * valid
            c = c + total
            for g, d in enumerate(_page_values(a, v_ref, C_ROWS)):
                accs[g] = accs[g] + d
        c_scr[...] = c
        for g in range(N_VGROUPS):
            acc_scr[g] = accs[g]

    @pl.when(p == 0)
    def _():
        c_scr[...] = jnp.zeros(c_scr.shape, F32)
        acc_scr[...] = jnp.zeros(acc_scr.shape, F32)
        pages([(kn_ref, vn_ref, valid_ref[...])])

    pages([(k_refs[g], v_refs[g], None) for g in range(PAGES_PER_STEP)])

    @pl.when(p == n_steps - 1)
    def _():
        o_ref[0] = acc_scr[...]


def _stick_sample(q, k, v, cache_k, cache_v, j, page_table, dec_batch):
    n_pool = cache_k.shape[1]
    n_pages = page_table.shape[1]
    n_steps = n_pages // PAGES_PER_STEP
    ck = _positions_minor(cache_k)
    cv = _positions_minor(cache_v)
    qt = _query_tiles(q, dec_batch, HEAD_DIM, C_ROWS, _c_row)
    kn = _new_page(k, dec_batch)
    vn = _new_page(v, dec_batch)
    valid = np.zeros((C_QROWS, PAGE_SIZE), np.float32)
    for c in range(N_VGROUPS):
        for hl in range(HEADS_PER_GROUP):
            for t in range(DEC_SEQ):
                valid[c * C_ROWS + _c_row(hl, t), :t] = 1.0
    page = (1, WIDTH, PAGE_SIZE)

    def page_map(g):
        return lambda b, p, pt: (j * n_pool + pt[b, n_pages - 1 - (p * PAGES_PER_STEP + g)], 0, 0)

    per_b = lambda b, p, pt: (b, 0, 0)
    in_specs = [pl.BlockSpec((1, N_VGROUPS, C_ROWS, MXU_DIM), lambda b, p, pt: (b, 0, 0, 0))]
    in_specs += [pl.BlockSpec(page, page_map(g)) for g in range(PAGES_PER_STEP)] * 2
    in_specs += [pl.BlockSpec(page, per_b), pl.BlockSpec(page, per_b),
                 pl.BlockSpec((C_QROWS, PAGE_SIZE), lambda b, p, pt: (0, 0)),
                 pl.BlockSpec((PAGE_SIZE, PAGE_SIZE), lambda b, p, pt: (0, 0))]
    grid_spec = pltpu.PrefetchScalarGridSpec(
        num_scalar_prefetch=1,
        grid=(dec_batch, n_steps),
        in_specs=in_specs,
        out_specs=pl.BlockSpec((1, N_VGROUPS, C_ROWS, MXU_DIM), lambda b, p, pt: (b, 0, 0, 0)),
        scratch_shapes=[pltpu.VMEM((C_QROWS, 1), F32),
                        pltpu.VMEM((N_VGROUPS, C_ROWS, MXU_DIM), F32)])
    o = pl.pallas_call(
        functools.partial(_stick_sample_kernel, n_steps=n_steps),
        grid_spec=grid_spec,
        out_shape=jax.ShapeDtypeStruct((dec_batch, N_VGROUPS, C_ROWS, MXU_DIM), F32),
        compiler_params=_cparams(("parallel", "arbitrary")),
        name="stick_sample",
    )(page_table, qt, *([ck] * PAGES_PER_STEP), *([cv] * PAGES_PER_STEP), kn, vn, jnp.asarray(valid),
      _tri_later(PAGE_SIZE))
    return _diag_blocks(o, dec_batch)


PROMPT_TM = 1024


def _heads_last(xt, batch, seq, n_sub, sub_dim):
    return jnp.transpose(xt.reshape(batch, n_sub, sub_dim, seq), (0, 3, 1, 2))


def kernel(x_prompt, x_sample, cache_a_k_g0, cache_a_v_g0, cache_a_k_g1, cache_a_v_g1, cache_a_k_g2, cache_a_v_g2, cache_b_k, cache_b_v, cache_c_k, cache_c_v, page_table, norm_gain, rel_bias, w_in_a, w_out_a, q_gain_a, k_gain_a, w_in_b, w_out_b, q_gain_b, k_gain_b, lambda_q1, lambda_k1, lambda_q2, lambda_k2, sub_gain_b, w_in_c, w_out_c):
    batch, seq, _ = x_prompt.shape
    dec_batch = x_sample.shape[0]
    depth = norm_gain.shape[0]
    n_p = batch * seq
    n_s = dec_batch * DEC_SEQ
    caches_ak = (cache_a_k_g0, cache_a_k_g1, cache_a_k_g2)
    caches_av = (cache_a_v_g0, cache_a_v_g1, cache_a_v_g2)

    bias_tab = _bias_table(rel_bias)
    tabs_a = _dil_prompt_tables(bias_tab)
    tps_a, tn_a = _dil_sample_tables(bias_tab)
    ext_b = _diff_prompt_bias_rows(bias_tab, seq)
    bias_sb = _diff_sample_bias(bias_tab, page_table.shape[1] * PAGE_SIZE)

    y_p = x_prompt.reshape(n_p, D_MODEL)
    y_s = x_sample.reshape(n_s, D_MODEL)
    a_kp = [[] for _ in DILATED_CONFIGS]
    a_vp = [[] for _ in DILATED_CONFIGS]
    a_ks, a_vs = [], []
    b_kp, b_vp, b_ks, b_vs = [], [], [], []
    c_kp, c_vp, c_ks, c_vs = [], [], [], []
    no_t = ("n", "n", "n", "n")

    for i in range(depth):
        kind, j = i % 3, i // 3
        g = norm_gain[i][None, :]
        if kind == 0:
            w = w_in_a[j].astype(BF16)
            wq, wk, wv, wg = (w[:, :WIDTH], w[:, WIDTH:2 * WIDTH], w[:, 2 * WIDTH:3 * WIDTH], w[:, 3 * WIDTH:])
            gq = jnp.tile(q_gain_a[j], N_HEADS)[None, :]
            gk = jnp.tile(k_gain_a[j], N_HEADS)[None, :]
            w_out = w_out_a[j].astype(BF16)
            norms = (HEAD_DIM, HEAD_DIM, 0, 0)
            wa = [wq, wk, wv, wg]
            q, k, kt, v, vt, gate = _proj_in(y_p, g, wa, [gq, gk], norms, ("n", "nt", "nt", "n1"), PROMPT_TM, seq)
            os_, ls_ = [], []
            for gi in range(len(DILATED_CONFIGS)):
                o_g, l_g = _dil_prompt_group(q, k, v, tabs_a[gi], gi, batch, seq)
                os_.append(o_g)
                ls_.append(l_g)
            y_p = _proj_out(os_ + ls_, gate, y_p, w_out, PROMPT_TM)
            for gi, (window, _) in enumerate(DILATED_CONFIGS):
                keep = min(window, seq)
                fs = slice(gi * A_OUT, (gi + 1) * A_OUT)
                a_kp[gi].append(_heads_last(kt[:, fs, seq - keep:], batch, keep, H_GROUP, HEAD_DIM))
                a_vp[gi].append(_heads_last(vt[:, fs, seq - keep:], batch, keep, H_GROUP, HEAD_DIM))
            q, k, v, gate = _proj_in(y_s, g, wa, [gq, gk], norms, ("n", "n", "n", "n1"), n_s, n_s)
            o = _dil_sample(q, k, v, caches_ak, caches_av, j, tps_a, tn_a, dec_batch)
            y_s = _proj_out([o], gate, y_s, w_out, n_s)
            a_ks.append(k.reshape(dec_batch, DEC_SEQ, N_HEADS, HEAD_DIM))
            a_vs.append(v.reshape(dec_batch, DEC_SEQ, N_HEADS, HEAD_DIM))
        else:
            w_in = (w_in_b if kind == 1 else w_in_c)[j].astype(BF16)
            w_out = (w_out_b if kind == 1 else w_out_c)[j].astype(BF16)
            ws = [w_in[:, c * WIDTH:(c + 1) * WIDTH] for c in range(4)]
            ws_p = [ws[0].T, ws[1].T, ws[2].T, ws[3]]
            or_p = ("t", "t", "t", "n")
            if kind == 1:
                lambda_init = float(0.8 - 0.6 * math.exp(-0.3 * i))
                gq = jnp.tile(q_gain_b[j], 2 * N_HEADS)[None, :]
                gk = jnp.tile(k_gain_b[j], 2 * N_HEADS)[None, :]
                groups = (DIFF_DIM, DIFF_DIM, 0, 0)
                gains_p, gains_s = [gq.T, gk.T], [gq, gk]
                lam_vecs = jnp.stack([lambda_q1[j], lambda_k1[j], lambda_q2[j], lambda_k2[j]])
                sub_gain = sub_gain_b[j][None, :]
            else:
                groups = (0, 0, 0, 0)
                gains_p, gains_s = [], []
            qt, kt, vt, gate = _proj_in(y_p, g, ws_p, gains_p, groups, or_p, PROMPT_TM, seq)
            qs, ks_, vs_, gate_s = _proj_in(y_s, g, ws, gains_s, groups, no_t, n_s, n_s)
            if kind == 1:
                o = _diff_prompt(qt, kt, vt, ext_b, lam_vecs, sub_gain.T, lambda_init, batch, seq)
                o_s = _diff_sample(qs, ks_, vs_, cache_b_k, cache_b_v, j, page_table, bias_sb, lam_vecs,
                                   sub_gain, lambda_init, dec_batch)
                b_kp.append(_heads_last(kt, batch, seq, 2 * N_HEADS, DIFF_DIM))
                b_vp.append(_heads_last(vt, batch, seq, N_HEADS, HEAD_DIM))
                b_ks.append(ks_.reshape(dec_batch, DEC_SEQ, 2 * N_HEADS, DIFF_DIM))
                b_vs.append(vs_.reshape(dec_batch, DEC_SEQ, N_HEADS, HEAD_DIM))
            else:
                o = _stick_prompt(qt, kt, vt, batch, seq)
                o_s = _stick_sample(qs, ks_, vs_, cache_c_k, cache_c_v, j, page_table, dec_batch)
                c_kp.append(_heads_last(kt, batch, seq, N_HEADS, HEAD_DIM))
                c_vp.append(_heads_last(vt, batch, seq, N_HEADS, HEAD_DIM))
                c_ks.append(ks_.reshape(dec_batch, DEC_SEQ, N_HEADS, HEAD_DIM))
                c_vs.append(vs_.reshape(dec_batch, DEC_SEQ, N_HEADS, HEAD_DIM))
            y_p = _proj_out([o], gate, y_p, w_out, PROMPT_TM)
            y_s = _proj_out([o_s], gate_s, y_s, w_out, n_s)

    return (y_p.reshape(batch, seq, D_MODEL), y_s.reshape(dec_batch, DEC_SEQ, D_MODEL),
            jnp.stack(a_kp[0]), jnp.stack(a_vp[0]), jnp.stack(a_kp[1]), jnp.stack(a_vp[1]),
            jnp.stack(a_kp[2]), jnp.stack(a_vp[2]), jnp.stack(a_ks), jnp.stack(a_vs),
            jnp.stack(b_kp), jnp.stack(b_vp), jnp.stack(b_ks), jnp.stack(b_vs),
            jnp.stack(c_kp), jnp.stack(c_vp), jnp.stack(c_ks), jnp.stack(c_vs))
```

```python
import functools
import math

import numpy as np
import jax
import jax.numpy as jnp
from jax import lax
from jax.experimental import pallas as pl
from jax.experimental.pallas import tpu as pltpu

F32 = jnp.float32
BF16 = jnp.bfloat16

D_MODEL = 1024
N_HEADS = 24
HEAD_DIM = 64
WIDTH = N_HEADS * HEAD_DIM
H_GROUP = 8
A_OUT = H_GROUP * HEAD_DIM
DIFF_DIM = HEAD_DIM // 2
DILATED_CONFIGS = ((128, 1), (512, 4), (2048, 16))
NUM_BUCKETS = 32
REL_MAX_DIST = 2048
EPS = 1e-6
PAGE_SIZE = 128
DEC_SEQ = 4
NEG = -1e30

LANES = 128
MXU_DIM = 256
VMEM_LIMIT = 52 * 1024 * 1024

BIAS_TABLE_LEN = 8320


def _cparams(sem):
    return pltpu.CompilerParams(dimension_semantics=sem, vmem_limit_bytes=VMEM_LIMIT)


def _nt_dot(a, b):
    return lax.dot_general(a, b, (((1,), (1,)), ((), ())), preferred_element_type=F32)


def _dot(a, b):
    return jnp.dot(a, b, preferred_element_type=F32)


def _split_bf16(x):
    hi = x.astype(BF16)
    lo = (x - hi.astype(F32)).astype(BF16)
    return hi, lo


def _positions_minor(cache):
    n, pool, pos, heads, dim = cache.shape
    return jnp.transpose(cache, (0, 1, 3, 4, 2)).reshape(n * pool, heads * dim, pos)


def _bucket_thresholds():
    max_exact = NUM_BUCKETS // 2
    n = np.arange(1, 1 << 14, dtype=np.float64)
    large = max_exact + (np.log(n / max_exact) / math.log(REL_MAX_DIST / max_exact)
                         * (NUM_BUCKETS - max_exact)).astype(np.int64)
    return tuple(int(n[large >= max_exact + k][0]) for k in range(1, NUM_BUCKETS - max_exact))


_BUCKET_THRESHOLDS = _bucket_thresholds()


def _bias_table_kernel(bt_ref, o_ref):
    max_exact = NUM_BUCKETS // 2
    n = lax.broadcasted_iota(jnp.int32, (N_HEADS, BIAS_TABLE_LEN), 1)
    large = jnp.full(n.shape, max_exact, jnp.int32)
    for thr in _BUCKET_THRESHOLDS:
        large = large + (n >= thr).astype(jnp.int32)
    bucket = jnp.where(n < max_exact, n, jnp.minimum(large, NUM_BUCKETS - 1))
    bt = bt_ref[...]
    acc = jnp.zeros(n.shape, F32)
    for b in range(NUM_BUCKETS):
        acc = jnp.where(bucket == b, jnp.broadcast_to(bt[:, b:b + 1], n.shape), acc)
    o_ref[...] = acc


def _bias_table(rel_bias):
    return pl.pallas_call(
        _bias_table_kernel,
        out_shape=jax.ShapeDtypeStruct((N_HEADS, BIAS_TABLE_LEN), F32),
        name="bias_table",
    )(rel_bias.T)


def _toeplitz(v, n_rows, n_cols):
    h = v.shape[0]
    p = n_rows + n_cols + 1
    pad = jnp.zeros((h, 1), v.dtype)
    w = jnp.concatenate([pad, v[:, ::-1], pad], axis=1)
    x = jnp.tile(w[:, None, :], (1, n_rows, 1)).reshape(h, n_rows * p)
    x = x[:, :n_rows * (p - 1)].reshape(h, n_rows, p - 1)
    return x[:, :, n_rows:n_rows + n_cols]


PROJ_TN = 512


def _group_sumsq(acc, bd, axis):
    hi, lo = _split_bf16(acc * acc)
    parts = []
    for c in range(PROJ_TN // MXU_DIM):
        sl = slice(c * MXU_DIM, (c + 1) * MXU_DIM)
        if axis == 1:
            parts.append(_dot(hi[:, sl], bd) + _dot(lo[:, sl], bd))
        else:
            parts.append(_dot(bd, hi[sl, :]) + _dot(bd, lo[sl, :]))
    return jnp.concatenate(parts, axis=axis)


def _proj_in_kernel(*refs, groups, orient, post):
    n_w = len(groups)
    n_g = sum(1 for gs in groups if gs)
    x_ref, g_ref = refs[0], refs[1]
    w_refs = refs[2:2 + n_w]
    gain_refs = list(refs[2 + n_w:2 + n_w + n_g])
    bd_ref = refs[2 + n_w + n_g]
    out_refs = list(refs[3 + n_w + n_g:-1])
    h_scr = refs[-1]

    @pl.when(pl.program_id(1) == 0)
    def _():
        x = x_ref[...]
        ms = jnp.mean(x * x, axis=-1, keepdims=True)
        h_scr[...] = (x * lax.rsqrt(ms + EPS) * g_ref[...]).astype(BF16)

    h = h_scr[...]
    for k in range(n_w):
        gs = groups[k]
        if orient[k] == "n1":
            out_ref = out_refs.pop(0)

            @pl.when(pl.program_id(1) == 0)
            def _(w_ref=w_refs[k], out_ref=out_ref):
                out_ref[...] = _dot(h_scr[...], w_ref[...])
            continue
        if orient[k] == "t":
            acc = _nt_dot(w_refs[k][...], h)
        else:
            acc = _dot(h, w_refs[k][...])
        if gs:
            bd = bd_ref[0 if gs == HEAD_DIM else 1]
            ss = _group_sumsq(acc, bd, 0 if orient[k] == "t" else 1)
            acc = acc * lax.rsqrt(ss * (1.0 / gs) + EPS) * gain_refs.pop(0)[...]
        if post[k] is not None:
            acc = (acc * post[k]).astype(BF16)
        if orient[k] == "t":
            out_refs.pop(0)[0] = acc
        else:
            out_refs.pop(0)[...] = acc
            if orient[k] == "nt":
                out_refs.pop(0)[0] = acc.T


def _block_diag_ones():
    r = np.arange(MXU_DIM)
    b64 = (r[:, None] // HEAD_DIM == r[None, :] // HEAD_DIM)
    b32 = (r[:, None] // DIFF_DIM == r[None, :] // DIFF_DIM)
    return jnp.asarray(np.stack([b64, b32]).astype(np.float32), dtype=BF16)


def _proj_in(x, g, weights, gains, groups, orient, tm, seq, post=None):
    post = post or (None,) * len(weights)
    assert all(p is None or o == "t" for p, o in zip(post, orient))
    m, d = x.shape
    tn = PROJ_TN
    n_w = len(weights)
    w_dim = weights[0].shape[0] if orient[0] == "t" else weights[0].shape[1]
    spb = seq // tm
    in_specs = [pl.BlockSpec((tm, d), lambda i, j: (i, 0)),
                pl.BlockSpec((1, d), lambda i, j: (0, 0))]
    for k in range(n_w):
        if orient[k] == "t":
            in_specs.append(pl.BlockSpec((tn, d), lambda i, j: (j, 0)))
        elif orient[k] == "n1":
            assert weights[k].shape[1] == tn and not groups[k]
            in_specs.append(pl.BlockSpec((d, tn), lambda i, j: (0, 0)))
        else:
            in_specs.append(pl.BlockSpec((d, tn), lambda i, j: (0, j)))
    for k in range(n_w):
        if groups[k]:
            if orient[k] == "t":
                in_specs.append(pl.BlockSpec((tn, 1), lambda i, j: (j, 0)))
            else:
                in_specs.append(pl.BlockSpec((1, tn), lambda i, j: (0, j)))
    in_specs.append(pl.BlockSpec((2, MXU_DIM, MXU_DIM), lambda i, j: (0, 0, 0)))
    out_specs, out_shape = [], []
    for k in range(n_w):
        if orient[k] == "n1":
            out_specs.append(pl.BlockSpec((tm, tn), lambda i, j: (i, 0)))
            out_shape.append(jax.ShapeDtypeStruct((m, tn), F32))
            continue
        if "n" in orient[k]:
            out_specs.append(pl.BlockSpec((tm, tn), lambda i, j: (i, j)))
            out_shape.append(jax.ShapeDtypeStruct((m, w_dim), F32))
        if "t" in orient[k]:
            out_specs.append(pl.BlockSpec((1, tn, tm), lambda i, j: (i // spb, j, i % spb)))
            out_shape.append(jax.ShapeDtypeStruct((m // seq, w_dim, seq), F32 if post[k] is None else BF16))
    return pl.pallas_call(
        functools.partial(_proj_in_kernel, groups=tuple(groups), orient=tuple(orient), post=tuple(post)),
        grid=(m // tm, w_dim // tn),
        in_specs=in_specs,
        out_specs=out_specs,
        out_shape=out_shape,
        scratch_shapes=[pltpu.VMEM((tm, d), BF16)],
        compiler_params=_cparams(("parallel", "arbitrary")),
        name="proj_in",
    )(x, g, *weights, *gains, _block_diag_ones())


def _proj_out_kernel(*refs, merge):
    if merge:
        o0, o1, o2, l0, l1, l2, gate_ref, x_ref, w_ref, out_ref = refs
        la, lb, lc = l0[...], l1[...], l2[...]
        m = jnp.maximum(jnp.maximum(la, lb), lc)
        ea, eb, ec = jnp.exp(la - m), jnp.exp(lb - m), jnp.exp(lc - m)
        o = (ea * o0[...] + eb * o1[...] + ec * o2[...]) / (ea + eb + ec)
    else:
        o_ref, gate_ref, x_ref, w_ref, out_ref = refs
        o = o_ref[...]
    gate = gate_ref[...]
    act = o * (gate / (1.0 + jnp.exp(-gate)))
    out_ref[...] = x_ref[...] + _dot(act.astype(BF16), w_ref[...])


def _proj_out(o_list, gate, x, w, tm):
    m, wd = gate.shape
    d = x.shape[1]
    merge = len(o_list) > 1
    row = lambda i: (i, 0)
    in_specs = [pl.BlockSpec((tm, wd), row) for _ in o_list]
    in_specs += [pl.BlockSpec((tm, wd), row), pl.BlockSpec((tm, d), row),
                 pl.BlockSpec((wd, d), lambda i: (0, 0))]
    return pl.pallas_call(
        functools.partial(_proj_out_kernel, merge=merge),
        grid=(m // tm,),
        in_specs=in_specs,
        out_specs=pl.BlockSpec((tm, d), row),
        out_shape=jax.ShapeDtypeStruct((m, d), F32),
        compiler_params=_cparams(("parallel",)),
        name="proj_out",
    )(*o_list, gate, x, w)


A_BLK = 128


A_HEADS = LANES // HEAD_DIM
A_UNROLL = 4


def _tn_dot(a, b):
    return lax.dot_general(a, b, (((0,), (0,)), ((), ())), preferred_element_type=F32)


def _dil_prompt_kernel(*refs, dil, use_prev):
    if use_prev:
        q_ref, kc_ref, vc_ref, kp_ref, vp_ref, t_ref, o_ref, l_ref = refs
    else:
        q_ref, kc_ref, vc_ref, t_ref, o_ref, l_ref = refs
    scale = HEAD_DIM ** -0.5
    first = jnp.minimum(pl.program_id(2), 1)

    def residues(chains):
        heads = [(i, h) for i in range(len(chains)) for h in range(A_HEADS)]
        rows = [c[0] for c in chains]
        tab = [c[2] for c in chains]
        q2 = [(q_ref[0, rw, :] * scale).astype(BF16) for rw in rows]
        kc2 = [kc_ref[0, rw, :].astype(BF16) for rw in rows]
        sl = [slice(HEAD_DIM * h, HEAD_DIM * (h + 1)) for h in range(A_HEADS)]
        st = [_nt_dot(kc2[i][:, sl[h]], q2[i][:, sl[h]]) + t_ref[tab[i], h, A_BLK:, :] for i, h in heads]
        m = [jnp.max(s, axis=0, keepdims=True) for s in st]
        if use_prev:
            kp2 = [(kp_ref if c[1][0] else kc_ref)[0, c[1][1], :].astype(BF16) for c in chains]
            sp = [_nt_dot(kp2[i][:, sl[h]], q2[i][:, sl[h]]) + t_ref[tab[i], h, :A_BLK, :] for i, h in heads]
            m = [jnp.maximum(a, jnp.max(s, axis=0, keepdims=True)) for a, s in zip(m, sp)]
        e = [jnp.exp(s - a) for s, a in zip(st, m)]
        l = [jnp.sum(x, axis=0, keepdims=True) for x in e]
        vc2 = [vc_ref[0, rw, :].astype(BF16) for rw in rows]
        o = [_tn_dot(vc2[i][:, sl[h]], e[n].astype(BF16)) for n, (i, h) in enumerate(heads)]
        if use_prev:
            ep = [jnp.exp(s - a) for s, a in zip(sp, m)]
            l = [a + jnp.sum(x, axis=0, keepdims=True) for a, x in zip(l, ep)]
            vp2 = [(vp_ref if c[1][0] else vc_ref)[0, c[1][1], :].astype(BF16) for c in chains]
            o = [a + _tn_dot(vp2[i][:, sl[h]], ep[n].astype(BF16)) for n, ((i, h), a) in enumerate(zip(heads, o))]
        o = [a / b for a, b in zip(o, l)]
        lse = [jnp.broadcast_to(a + jnp.log(b), (HEAD_DIM, A_BLK)) for a, b in zip(m, l)]
        for i, rw in enumerate(rows):
            o_ref[0, rw, :] = jnp.concatenate(o[A_HEADS * i:A_HEADS * (i + 1)], axis=0).T
            l_ref[0, rw, :] = jnp.concatenate(lse[A_HEADS * i:A_HEADS * (i + 1)], axis=0).T

    def strided(r):
        rw = pl.ds(r, A_BLK, stride=dil)
        return (rw, (True, rw), first)

    if dil == 1:
        chains = [(pl.ds(0, A_BLK), (True, pl.ds(0, A_BLK)), first)]
        chains += [(pl.ds(A_BLK * j, A_BLK), (False, pl.ds(A_BLK * (j - 1), A_BLK)), 1) for j in range(1, A_UNROLL)]
        residues(chains)
    elif dil <= A_UNROLL:
        residues([strided(r) for r in range(dil)])
    else:
        def body(i, carry):
            residues([strided(i * A_UNROLL + u) for u in range(A_UNROLL)])
            return carry
        lax.fori_loop(0, dil // A_UNROLL, body, 0)


def _group_bias(bias_tab, g):
    window, dil = DILATED_CONFIGS[g]
    n_keys = window // dil + 1
    return bias_tab[g * H_GROUP:(g + 1) * H_GROUP, 0:n_keys * dil:dil]


def _dil_prompt_tables(bias_tab):
    tables = []
    for g in range(len(DILATED_CONFIGS)):
        bj = _group_bias(bias_tab, g)
        lo, hi = A_BLK - (2 * A_BLK - 1), A_BLK + A_BLK - 1
        v = jnp.full((H_GROUP, hi - lo + 1), NEG, F32)
        v = lax.dynamic_update_slice(v, bj, (0, -lo))
        t_gen = _toeplitz(v, A_BLK, 2 * A_BLK)
        t_first = t_gen.at[:, :, :A_BLK].set(NEG)
        tables.append(jnp.swapaxes(jnp.stack([t_first, t_gen]), 2, 3))
    return tables


def _dil_prompt_group(q, k, v, table, g, batch, seq):
    dil = DILATED_CONFIGS[g][1]
    rows = A_BLK * (dil if dil > 1 else A_UNROLL)
    prev_rows = rows if dil > 1 else A_BLK
    nqb = seq // rows
    use_prev = nqb > 1
    n_hp = A_OUT // LANES
    q3, k3, v3 = (a.reshape(batch, seq, WIDTH) for a in (q, k, v))
    cur = lambda b, h, i: (b, i, g * n_hp + h)
    prev = lambda b, h, i: (b, jnp.maximum(i * (rows // prev_rows) - 1, 0), g * n_hp + h)
    blk = (1, rows, LANES)
    in_specs = [pl.BlockSpec(blk, cur), pl.BlockSpec(blk, cur), pl.BlockSpec(blk, cur)]
    args = [q3, k3, v3]
    if use_prev:
        in_specs += [pl.BlockSpec((1, prev_rows, LANES), prev)] * 2
        args += [k3, v3]
    in_specs += [pl.BlockSpec((2, A_HEADS, 2 * A_BLK, A_BLK), lambda b, h, i: (0, h, 0, 0))]
    args += [table]
    out_map = lambda b, h, i: (b, i, h)
    o, lse = pl.pallas_call(
        functools.partial(_dil_prompt_kernel, dil=dil, use_prev=use_prev),
        grid=(batch, n_hp, nqb),
        in_specs=in_specs,
        out_specs=[pl.BlockSpec(blk, out_map), pl.BlockSpec(blk, out_map)],
        out_shape=[jax.ShapeDtypeStruct((batch, seq, A_OUT), F32)] * 2,
        compiler_params=_cparams(("parallel", "parallel", "arbitrary")),
        name="dilated_prompt_g%d" % g,
    )(*args)
    return o.reshape(batch * seq, A_OUT), lse.reshape(batch * seq, A_OUT)


NEW_PAD = 16
A_ROWS = H_GROUP * DEC_SEQ


def _dil_sample_kernel(qbd_ref, kn_ref, vn_ref, k0, v0, k1, v1, k2, v2, tp0, tp1, tp2, tn_ref, o_ref):
    scale = HEAD_DIM ** -0.5
    outs, lses = [], []
    for g, (kr, vr, tp) in enumerate(((k0, v0, tp0), (k1, v1, tp1), (k2, v2, tp2))):
        gs = slice(A_OUT * g, A_OUT * (g + 1))
        qb = qbd_ref[0, g]
        s = _dot(qb, kr[0].astype(BF16)) * scale + tp[...]
        sn = _nt_dot(qb, kn_ref[0, :, gs].astype(BF16)) * scale + tn_ref[g]
        m = jnp.maximum(jnp.max(s, axis=1, keepdims=True), jnp.max(sn, axis=1, keepdims=True))
        e = jnp.exp(s - m)
        en = jnp.exp(sn - m)
        l = jnp.sum(e, axis=1, keepdims=True) + jnp.sum(en, axis=1, keepdims=True)
        o = _nt_dot(e.astype(BF16), vr[0].astype(BF16)) + _dot(en.astype(BF16), vn_ref[0, :, gs].astype(BF16))
        outs.append(o / l)
        lses.append(m + jnp.log(l))
    mm = jnp.maximum(jnp.maximum(lses[0], lses[1]), lses[2])
    es = [jnp.exp(x - mm) for x in lses]
    o_ref[0] = (es[0] * outs[0] + es[1] * outs[1] + es[2] * outs[2]) / (es[0] + es[1] + es[2])


def _dil_sample_tables(bias_tab):
    tps, tns = [], []
    tq = np.arange(DEC_SEQ)
    for g, (window, dil) in enumerate(DILATED_CONFIGS):
        bg = bias_tab[g * H_GROUP:(g + 1) * H_GROUP]
        length = window
        pos = np.arange(length)
        dist = length + tq[:, None] - pos[None, :]
        ok = (dist % dil == 0) & (dist <= window)
        rows = jnp.stack([bg[:, t + 1:t + 1 + length][:, ::-1] for t in range(DEC_SEQ)], axis=1)
        tps.append(jnp.where(jnp.asarray(ok)[None], rows, NEG).reshape(A_ROWS, length))
        tn_ = np.arange(NEW_PAD)
        dn = tq[:, None] - tn_[None, :]
        okn = (dn >= 0) & (tn_[None, :] < DEC_SEQ) & (dn % dil == 0)
        vals = bg[:, np.clip(dn, 0, DEC_SEQ)]
        tns.append(jnp.where(jnp.asarray(okn)[None], vals, NEG).reshape(A_ROWS, NEW_PAD))
    return tps, jnp.stack(tns)


def _pad_new(x, dec_batch, rows):
    x = x.reshape(dec_batch, DEC_SEQ, x.shape[-1])
    return jnp.pad(x, ((0, 0), (0, rows - DEC_SEQ), (0, 0)))


def _dil_sample(q, k, v, caches_k, caches_v, j, tps, tn, dec_batch):
    n_grp = len(DILATED_CONFIGS)
    sel = np.zeros((H_GROUP, DEC_SEQ, A_ROWS), np.float32)
    for h in range(H_GROUP):
        for t in range(DEC_SEQ):
            sel[h, t, h * DEC_SEQ + t] = 1.0
    q5 = q.reshape(dec_batch, DEC_SEQ, n_grp, H_GROUP, HEAD_DIM)
    qbd = jnp.einsum("btghd,htr->bgrhd", q5, jnp.asarray(sel), precision=lax.Precision.HIGHEST)
    qbd = qbd.reshape(dec_batch, n_grp, A_ROWS, A_OUT).astype(BF16)
    kn, vn = (_pad_new(a, dec_batch, NEW_PAD) for a in (k, v))
    args = [qbd, kn, vn]
    in_specs = [pl.BlockSpec((1, n_grp, A_ROWS, A_OUT), lambda b: (b, 0, 0, 0)),
                pl.BlockSpec((1, NEW_PAD, WIDTH), lambda b: (b, 0, 0)),
                pl.BlockSpec((1, NEW_PAD, WIDTH), lambda b: (b, 0, 0))]
    for g, (window, dil) in enumerate(DILATED_CONFIGS):
        for c in (caches_k[g], caches_v[g]):
            args.append(_positions_minor(c))
            in_specs.append(pl.BlockSpec((1, A_OUT, window), lambda b, j=j: (j * dec_batch + b, 0, 0)))
    for tp in tps:
        args.append(tp)
        in_specs.append(pl.BlockSpec(tp.shape, lambda b: (0, 0)))
    args.append(tn)
    in_specs.append(pl.BlockSpec(tn.shape, lambda b: (0, 0, 0)))
    o = pl.pallas_call(
        _dil_sample_kernel,
        grid=(dec_batch,),
        in_specs=in_specs,
        out_specs=pl.BlockSpec((1, A_ROWS, A_OUT), lambda b: (b, 0, 0)),
        out_shape=jax.ShapeDtypeStruct((dec_batch, A_ROWS, A_OUT), F32),
        compiler_params=_cparams(("parallel",)),
        name="dilated_sample",
    )(*args)
    o = o.reshape(dec_batch, H_GROUP, DEC_SEQ, H_GROUP, HEAD_DIM)
    idx = np.arange(H_GROUP)
    o = o[:, idx, :, idx, :]
    return jnp.transpose(o, (1, 2, 0, 3)).reshape(dec_batch * DEC_SEQ, A_OUT)


BC_BLK = 256
HEADS_PER_STEP = MXU_DIM // HEAD_DIM


def _lambda_value(lam_ref, lambda_init):
    lv = lam_ref[...]
    s1 = jnp.sum(lv[0:1] * lv[1:2], axis=1, keepdims=True)
    s2 = jnp.sum(lv[2:3] * lv[3:4], axis=1, keepdims=True)
    return jnp.exp(s1) - jnp.exp(s2) + lambda_init


LOG2E = 1.4426950408889634


def _diff_prompt_kernel(qt_ref, k_ref, vt_ref, ext_ref, lam_ref, sg_ref, o_ref, tab_scr, st_scr, p_scr, *,
                        lambda_init, nqb):
    b = pl.program_id(1)
    qi = pl.program_id(2)

    @pl.when((b == 0) & (qi == 0))
    def _():
        for hl in range(HEADS_PER_STEP):
            for dd in range(nqb):
                row = ext_ref[0, hl:hl + 1, dd * BC_BLK:(dd + 2) * BC_BLK] * LOG2E
                x = jnp.broadcast_to(row, (BC_BLK, 2 * BC_BLK))
                tab_scr[hl, dd] = pltpu.roll(x, 0, 1, stride=1, stride_axis=0)[:, BC_BLK:]

    streams = [(hl, half) for hl in range(HEADS_PER_STEP) for half in range(2)]
    qts = []
    for hl, half in streams:
        c0 = HEAD_DIM * hl + DIFF_DIM * half
        qts.append(qt_ref[0, c0:c0 + DIFF_DIM, :])

    def body(kb, carry):
        return update(carry, [kb])

    def update(carry, kbs):
        ms, ls, accs = carry
        keys = [pl.ds(pl.multiple_of(kb * BC_BLK, BC_BLK), BC_BLK) for kb in kbs]
        ms_n, ls_n, accs_n, alphas = [], [], [], []
        for s, (hl, half) in enumerate(streams):
            c0 = HEAD_DIM * hl + DIFF_DIM * half
            m = ms[s]
            for j, kb in enumerate(kbs):
                ks = k_ref[0, c0:c0 + DIFF_DIM, keys[j]].astype(BF16)
                st = _tn_dot(ks, qts[s]) + tab_scr[hl, qi - kb]
                st_scr[j, s] = st
                m = jnp.maximum(m, jnp.max(st, axis=0, keepdims=True))
            ms_n.append(m)
        for s in range(len(streams)):
            alphas.append(jnp.exp2(ms[s] - ms_n[s]))
            l = alphas[s] * ls[s]
            for j in range(len(kbs)):
                p = jnp.exp2(st_scr[j, s] - ms_n[s])
                l = l + jnp.sum(p, axis=0, keepdims=True)
                p_scr[j, s] = p.astype(BF16)
            ls_n.append(l)
        for s, (hl, half) in enumerate(streams):
            acc = alphas[s] * accs[s]
            for j in range(len(kbs)):
                vt = vt_ref[0, HEAD_DIM * hl:HEAD_DIM * (hl + 1), keys[j]].astype(BF16)
                acc = acc + _dot(vt, p_scr[j, s])
            accs_n.append(acc)
        return tuple(ms_n), tuple(ls_n), tuple(accs_n)

    n_s = len(streams)
    init = (tuple(jnp.full((1, BC_BLK), NEG, F32) for _ in range(n_s)),
            tuple(jnp.zeros((1, BC_BLK), F32) for _ in range(n_s)),
            tuple(jnp.zeros((HEAD_DIM, BC_BLK), F32) for _ in range(n_s)))
    n_kb = qi + 1
    carry = lax.fori_loop(0, n_kb // 2, lambda i, c: update(c, [2 * i, 2 * i + 1]), init)
    _, ls, accs = lax.cond(n_kb % 2 == 1, lambda c: body(qi, c), lambda c: c, carry)

    lam = _lambda_value(lam_ref, lambda_init)
    outs = []
    for hl in range(HEADS_PER_STEP):
        o = accs[2 * hl] / ls[2 * hl] - lam * (accs[2 * hl + 1] / ls[2 * hl + 1])
        ms = jnp.mean(o * o, axis=0, keepdims=True)
        outs.append(o * lax.rsqrt(ms + EPS) * sg_ref[...] * (1.0 - lambda_init))
    o_ref[0] = jnp.concatenate(outs, axis=0).T


def _diff_prompt_bias_rows(bias_tab, seq):
    ext = jnp.concatenate([jnp.full((N_HEADS, BC_BLK), NEG, F32), bias_tab[:, :seq + BC_BLK]], axis=1)
    return ext.reshape(N_HEADS // HEADS_PER_STEP, HEADS_PER_STEP, seq + 2 * BC_BLK)


def _diff_prompt(qt, kt, vt, ext, lam_vecs, sub_gain_col, lambda_init, batch, seq):
    nqb = seq // BC_BLK
    n_hg = WIDTH // MXU_DIM
    o = pl.pallas_call(
        functools.partial(_diff_prompt_kernel, lambda_init=lambda_init, nqb=nqb),
        grid=(n_hg, batch, nqb),
        in_specs=[pl.BlockSpec((1, MXU_DIM, BC_BLK), lambda h, b, i: (b, h, i)),
                  pl.BlockSpec((1, MXU_DIM, seq), lambda h, b, i: (b, h, 0)),
                  pl.BlockSpec((1, MXU_DIM, seq), lambda h, b, i: (b, h, 0)),
                  pl.BlockSpec((1, HEADS_PER_STEP, seq + 2 * BC_BLK), lambda h, b, i: (h, 0, 0)),
                  pl.BlockSpec((4, DIFF_DIM), lambda h, b, i: (0, 0)),
                  pl.BlockSpec((HEAD_DIM, 1), lambda h, b, i: (0, 0))],
        out_specs=pl.BlockSpec((1, BC_BLK, MXU_DIM), lambda h, b, i: (b, i, h)),
        out_shape=jax.ShapeDtypeStruct((batch, seq, WIDTH), F32),
        scratch_shapes=[pltpu.VMEM((HEADS_PER_STEP, nqb, BC_BLK, BC_BLK), F32),
                        pltpu.VMEM((2, 2 * HEADS_PER_STEP, BC_BLK, BC_BLK), F32),
                        pltpu.VMEM((2, 2 * HEADS_PER_STEP, BC_BLK, BC_BLK), BF16)],
        compiler_params=_cparams(("arbitrary", "arbitrary", "arbitrary")),
        name="diff_prompt",
    )(qt, kt, vt, ext, lam_vecs, sub_gain_col)
    return o.reshape(batch * seq, WIDTH)


def _log_sigmoid(z):
    return jnp.minimum(z, 0.0) - jnp.log(1.0 + jnp.exp(-jnp.abs(z)))


def _suffix_sum_rows(x, tri):
    hi, lo = _split_bf16(x)
    return _dot(hi, tri) + _dot(lo, tri)


STICK_HEADS = MXU_DIM // HEAD_DIM
UNDERFLOW = 104.0


def _stick_prompt_kernel(qt_ref, k_ref, vt_ref, tri_ref, o_ref, lsc_scr, hilo_scr, a_scr):
    qi = pl.program_id(2)
    tri = tri_ref[...]
    key_i = lax.broadcasted_iota(jnp.int32, (BC_BLK, BC_BLK), 0)
    qry_i = lax.broadcasted_iota(jnp.int32, (BC_BLK, BC_BLK), 1)
    strict = key_i < qry_i
    qts = [qt_ref[0, HEAD_DIM * h:HEAD_DIM * (h + 1), :] for h in range(STICK_HEADS)]

    def block(kb, cs, accs, masked):
        keys = pl.ds(pl.multiple_of(kb * BC_BLK, BC_BLK), BC_BLK)
        cs_n, accs_n, edge = [], [], []

        def stage_a(h):
            fs = slice(HEAD_DIM * h, HEAD_DIM * (h + 1))
            z = _tn_dot(k_ref[0, fs, keys].astype(BF16), qts[h])
            neg_abs = lax.bitcast_convert_type(
                lax.bitcast_convert_type(z, jnp.uint32) | jnp.uint32(0x80000000), F32)
            t = jnp.log(1.0 + jnp.exp(neg_abs))
            ls = jnp.minimum(z, 0.0) - t
            stay = ls - z
            if masked:
                stay = jnp.where(strict, stay, 0.0)
            hi, lo = _split_bf16(stay)
            hilo_scr[h, :BC_BLK] = hi
            hilo_scr[h, BC_BLK:] = lo
            lsc_scr[h] = ls + cs[h]
            edge.append(stay[0:1] - ls[0:1])

        def stage_b(h):
            x = lsc_scr[h] + _dot(tri, hilo_scr[h])
            a = jnp.exp(x)
            if masked:
                a = jnp.where(strict, a, 0.0)
            a_scr[h] = a.astype(BF16)
            cs_n.append(x[0:1] + edge[h])

        def stage_c(h):
            fs = slice(HEAD_DIM * h, HEAD_DIM * (h + 1))
            accs_n.append(accs[h] + _dot(vt_ref[0, fs, keys].astype(BF16), a_scr[h]))

        for stage in (stage_a, stage_b, stage_c):
            for h in range(STICK_HEADS):
                stage(h)
        return tuple(cs_n), tuple(accs_n)

    def alive(cs):
        top = cs[0]
        for c in cs[1:]:
            top = jnp.maximum(top, c)
        return jnp.max(top) > -UNDERFLOW

    cs = tuple(jnp.zeros((1, BC_BLK), F32) for _ in range(STICK_HEADS))
    accs = tuple(jnp.zeros((HEAD_DIM, BC_BLK), F32) for _ in range(STICK_HEADS))
    cs, accs = block(qi, cs, accs, True)

    def cond(carry):
        kb, go, _, _ = carry
        return (kb >= 0) & go

    def body(carry):
        kb, _, cs, accs = carry
        cs, accs = block(kb, cs, accs, False)
        return kb - 1, alive(cs), cs, accs

    _, _, _, accs = lax.while_loop(cond, body, (qi - 1, alive(cs), cs, accs))
    o_ref[0] = jnp.concatenate(accs, axis=0).T


def _tri_later(n):
    r = np.arange(n)
    return jnp.asarray((r[:, None] > r[None, :]).astype(np.float32), dtype=BF16)


def _stick_prompt(qt, kt, vt, batch, seq):
    nqb = seq // BC_BLK
    n_hg = WIDTH // MXU_DIM
    o = pl.pallas_call(
        _stick_prompt_kernel,
        grid=(n_hg, batch, nqb),
        in_specs=[pl.BlockSpec((1, MXU_DIM, BC_BLK), lambda h, b, i: (b, h, i)),
                  pl.BlockSpec((1, MXU_DIM, seq), lambda h, b, i: (b, h, 0)),
                  pl.BlockSpec((1, MXU_DIM, seq), lambda h, b, i: (b, h, 0)),
                  pl.BlockSpec((BC_BLK, 2 * BC_BLK), lambda h, b, i: (0, 0))],
        out_specs=pl.BlockSpec((1, BC_BLK, MXU_DIM), lambda h, b, i: (b, i, h)),
        out_shape=jax.ShapeDtypeStruct((batch, seq, WIDTH), F32),
        scratch_shapes=[pltpu.VMEM((STICK_HEADS, BC_BLK, BC_BLK), F32),
                        pltpu.VMEM((STICK_HEADS, 2 * BC_BLK, BC_BLK), BF16),
                        pltpu.VMEM((STICK_HEADS, BC_BLK, BC_BLK), BF16)],
        compiler_params=_cparams(("parallel", "parallel", "arbitrary")),
        name="stick_prompt",
    )(qt, kt, vt, jnp.tile(_tri_later(BC_BLK).T, (1, 2)))
    return o.reshape(batch * seq, WIDTH)


HEADS_PER_GROUP = MXU_DIM // HEAD_DIM
N_VGROUPS = WIDTH // MXU_DIM
B_ROWS = 2 * HEADS_PER_GROUP * DEC_SEQ
C_ROWS = HEADS_PER_GROUP * DEC_SEQ
B_QROWS = N_VGROUPS * B_ROWS
C_QROWS = N_VGROUPS * C_ROWS
PAGES_PER_STEP = 8


def _page_scores(q_ref, k_page_ref):
    parts = []
    for c in range(N_VGROUPS):
        kt = k_page_ref[0, MXU_DIM * c:MXU_DIM * (c + 1), :].astype(BF16)
        parts.append(_dot(q_ref[0, c], kt))
    return jnp.concatenate(parts, axis=0)


def _page_values(w, v_page_ref, rows):
    outs = []
    for c in range(N_VGROUPS):
        vt = v_page_ref[0, MXU_DIM * c:MXU_DIM * (c + 1), :].astype(BF16)
        outs.append(_nt_dot(w[rows * c:rows * (c + 1), :].astype(BF16), vt))
    return outs


def _diff_sample_kernel(pt_ref, q_ref, *refs, lambda_init, n_steps):
    k_refs = refs[:PAGES_PER_STEP]
    v_refs = refs[PAGES_PER_STEP:2 * PAGES_PER_STEP]
    (kn_ref, vn_ref, bias_ref, biasn_ref, lam_ref, sg_ref, o_ref,
     s_scr, m_scr, l_scr, acc_scr) = refs[2 * PAGES_PER_STEP:]
    ph = pl.program_id(1)
    p = pl.program_id(2)
    scale = DIFF_DIM ** -0.5
    n_pages = n_steps * PAGES_PER_STEP

    def score_pages(pages):
        m = m_scr[...]
        for slot, k_ref, bias in pages:
            s = _page_scores(q_ref, k_ref) * scale + bias
            s_scr[slot] = s
            m = jnp.maximum(m, s)
        m_scr[...] = m

    def value_pages(pages):
        l = l_scr[...]
        accs = [acc_scr[c] for c in range(N_VGROUPS)]
        for slot, v_ref in pages:
            e = jnp.exp(s_scr[slot] - m_scr[...])
            l = l + e
            for c, d in enumerate(_page_values(e, v_ref, B_ROWS)):
                accs[c] = accs[c] + d
        l_scr[...] = l
        for c in range(N_VGROUPS):
            acc_scr[c] = accs[c]

    @pl.when((ph == 0) & (p == 0))
    def _():
        m_scr[...] = jnp.full(m_scr.shape, NEG, F32)
        l_scr[...] = jnp.zeros(l_scr.shape, F32)
        acc_scr[...] = jnp.zeros(acc_scr.shape, F32)

    @pl.when(ph == 0)
    def _():
        score_pages([(p * PAGES_PER_STEP + g, k_refs[g], bias_ref[:, PAGE_SIZE * g:PAGE_SIZE * (g + 1)])
                     for g in range(PAGES_PER_STEP)])

    @pl.when((ph == 0) & (p == n_steps - 1))
    def _():
        score_pages([(n_pages, kn_ref, biasn_ref[...])])

    @pl.when((ph == 1) & (p == 0))
    def _():
        m_scr[...] = jnp.broadcast_to(jnp.max(m_scr[...], axis=1, keepdims=True), m_scr.shape)

    @pl.when(ph == 1)
    def _():
        value_pages([(p * PAGES_PER_STEP + g, v_refs[g]) for g in range(PAGES_PER_STEP)])

    @pl.when((ph == 1) & (p == n_steps - 1))
    def _():
        value_pages([(n_pages, vn_ref)])
        lam = _lambda_value(lam_ref, lambda_init)
        l = jnp.sum(l_scr[...], axis=1, keepdims=True)
        half_rows = B_ROWS // 2
        r = lax.broadcasted_iota(jnp.int32, (half_rows, MXU_DIM), 0)
        ln = lax.broadcasted_iota(jnp.int32, (half_rows, MXU_DIM), 1)
        own = (r // DEC_SEQ) == (ln // HEAD_DIM)
        for c in range(N_VGROUPS):
            an = acc_scr[c] / l[B_ROWS * c:B_ROWS * (c + 1)]
            d = jnp.where(own, an[:half_rows] - lam * an[half_rows:], 0.0)
            ms = jnp.sum(d * d, axis=1, keepdims=True) * (1.0 / HEAD_DIM)
            o_ref[0, c] = d * lax.rsqrt(ms + EPS) * sg_ref[...] * (1.0 - lambda_init)


def _diag_blocks(x, dec_batch):
    x = x.reshape(dec_batch, N_VGROUPS, HEADS_PER_GROUP, DEC_SEQ, HEADS_PER_GROUP, HEAD_DIM)
    idx = np.arange(HEADS_PER_GROUP)
    x = x[:, :, idx, :, idx, :]
    x = jnp.transpose(x, (1, 3, 2, 0, 4))
    return x.reshape(dec_batch * DEC_SEQ, WIDTH)


def _query_tiles(q, dec_batch, sub_dim, rows, row_of):
    n_sub = MXU_DIM // sub_dim
    sel = np.zeros((n_sub, DEC_SEQ, rows), np.float32)
    for j in range(n_sub):
        for t in range(DEC_SEQ):
            sel[j, t, row_of(j, t)] = 1.0
    q5 = q.reshape(dec_batch, DEC_SEQ, N_VGROUPS, n_sub, sub_dim)
    qt = jnp.einsum("btcjd,jtr->bcrjd", q5, jnp.asarray(sel), precision=lax.Precision.HIGHEST)
    return qt.reshape(dec_batch, N_VGROUPS, rows, MXU_DIM).astype(BF16)


def _b_row(j, t):
    return (j % 2) * (B_ROWS // 2) + (j // 2) * DEC_SEQ + t


def _c_row(hl, t):
    return hl * DEC_SEQ + t


def _diff_sample_bias(bias_tab, past_len):
    segs = [bias_tab[:, t + 1:t + 1 + past_len][:, ::-1] for t in range(DEC_SEQ)]
    past = jnp.stack(segs, axis=1)
    tn_ = np.arange(PAGE_SIZE)[None, :]
    tq = np.arange(DEC_SEQ)[:, None]
    dist = tq - tn_
    ok = (dist >= 0) & (tn_ < DEC_SEQ)
    new = jnp.where(jnp.asarray(ok)[None], bias_tab[:, np.clip(dist, 0, DEC_SEQ)], NEG)
    full = jnp.concatenate([past, new], axis=2)
    n_cols = past_len + PAGE_SIZE
    full = full.reshape(N_VGROUPS, 1, HEADS_PER_GROUP, DEC_SEQ, n_cols)
    full = jnp.broadcast_to(full, (N_VGROUPS, 2, HEADS_PER_GROUP, DEC_SEQ, n_cols))
    return full.reshape(B_QROWS, n_cols)


def _new_page(x, dec_batch):
    x = jnp.transpose(x.reshape(dec_batch, DEC_SEQ, WIDTH), (0, 2, 1))
    return jnp.pad(x, ((0, 0), (0, 0), (0, PAGE_SIZE - DEC_SEQ)))


def _diff_sample(q, k, v, cache_k, cache_v, j, page_table, bias_sb, lam_vecs, sub_gain, lambda_init, dec_batch):
    n_pool = cache_k.shape[1]
    n_pages = page_table.shape[1]
    n_steps = n_pages // PAGES_PER_STEP
    ck = _positions_minor(cache_k)
    cv = _positions_minor(cache_v)
    qt = _query_tiles(q, dec_batch, DIFF_DIM, B_ROWS, _b_row)
    kn = _new_page(k, dec_batch)
    vn = _new_page(v, dec_batch)
    sg = jnp.tile(sub_gain, (1, HEADS_PER_GROUP))
    last = n_steps - 1
    page = (1, WIDTH, PAGE_SIZE)

    def k_map(g):
        return lambda b, ph, p, pt: (j * n_pool + pt[b, jnp.where(ph == 0, p, last) * PAGES_PER_STEP + g], 0, 0)

    def v_map(g):
        return lambda b, ph, p, pt: (j * n_pool + pt[b, jnp.where(ph == 1, p, 0) * PAGES_PER_STEP + g], 0, 0)

    per_b = lambda b, ph, p, pt: (b, 0, 0)
    in_specs = [pl.BlockSpec((1, N_VGROUPS, B_ROWS, MXU_DIM), lambda b, ph, p, pt: (b, 0, 0, 0))]
    in_specs += [pl.BlockSpec(page, k_map(g)) for g in range(PAGES_PER_STEP)]
    in_specs += [pl.BlockSpec(page, v_map(g)) for g in range(PAGES_PER_STEP)]
    in_specs += [pl.BlockSpec(page, per_b), pl.BlockSpec(page, per_b),
                 pl.BlockSpec((B_QROWS, PAGES_PER_STEP * PAGE_SIZE),
                              lambda b, ph, p, pt: (0, jnp.where(ph == 0, p, last))),
                 pl.BlockSpec((B_QROWS, PAGE_SIZE), lambda b, ph, p, pt: (0, n_pages)),
                 pl.BlockSpec((4, DIFF_DIM), lambda b, ph, p, pt: (0, 0)),
                 pl.BlockSpec((1, MXU_DIM), lambda b, ph, p, pt: (0, 0))]
    grid_spec = pltpu.PrefetchScalarGridSpec(
        num_scalar_prefetch=1,
        grid=(dec_batch, 2, n_steps),
        in_specs=in_specs,
        out_specs=pl.BlockSpec((1, N_VGROUPS, B_ROWS // 2, MXU_DIM), lambda b, ph, p, pt: (b, 0, 0, 0)),
        scratch_shapes=[pltpu.VMEM((n_pages + 1, B_QROWS, PAGE_SIZE), F32),
                        pltpu.VMEM((B_QROWS, PAGE_SIZE), F32),
                        pltpu.VMEM((B_QROWS, PAGE_SIZE), F32),
                        pltpu.VMEM((N_VGROUPS, B_ROWS, MXU_DIM), F32)])
    o = pl.pallas_call(
        functools.partial(_diff_sample_kernel, lambda_init=lambda_init, n_steps=n_steps),
        grid_spec=grid_spec,
        out_shape=jax.ShapeDtypeStruct((dec_batch, N_VGROUPS, B_ROWS // 2, MXU_DIM), F32),
        compiler_params=_cparams(("parallel", "arbitrary", "arbitrary")),
        name="diff_sample",
    )(page_table, qt, *([ck] * PAGES_PER_STEP), *([cv] * PAGES_PER_STEP), kn, vn, bias_sb, bias_sb, lam_vecs, sg)
    return _diag_blocks(o, dec_batch)


def _stick_sample_kernel(pt_ref, q_ref, *refs, n_steps):
    k_refs = refs[:PAGES_PER_STEP]
    v_refs = refs[PAGES_PER_STEP:2 * PAGES_PER_STEP]
    kn_ref, vn_ref, valid_ref, tri_ref, o_ref, c_scr, acc_scr = refs[2 * PAGES_PER_STEP:]
    p = pl.program_id(1)
    scale = HEAD_DIM ** -0.5

    def pages(items):
        c = c_scr[...]
        accs = [acc_scr[g] for g in range(N_VGROUPS)]
        parts = []
        for k_ref, v_ref, valid in items:
            z = _page_scores(q_ref, k_ref) * scale
            ls = _log_sigmoid(z)
            stay = ls - z
            if valid is not None:
                stay = stay * valid
            parts.append((ls, stay, jnp.sum(stay, axis=1, keepdims=True)))
        for (k_ref, v_ref, valid), (ls, stay, total) in zip(items, parts):
            after = _suffix_sum_rows(stay, tri_ref[...])
            a = jnp.exp(ls + after + c)
            if valid is not None:
                a = a * valid
            c = c + total
            for g, d in enumerate(_page_values(a, v_ref, C_ROWS)):
                accs[g] = accs[g] + d
        c_scr[...] = c
        for g in range(N_VGROUPS):
            acc_scr[g] = accs[g]

    @pl.when(p == 0)
    def _():
        c_scr[...] = jnp.zeros(c_scr.shape, F32)
        acc_scr[...] = jnp.zeros(acc_scr.shape, F32)
        pages([(kn_ref, vn_ref, valid_ref[...])])

    pages([(k_refs[g], v_refs[g], None) for g in range(PAGES_PER_STEP)])

    @pl.when(p == n_steps - 1)
    def _():
        o_ref[0] = acc_scr[...]


def _stick_sample(q, k, v, cache_k, cache_v, j, page_table, dec_batch):
    n_pool = cache_k.shape[1]
    n_pages = page_table.shape[1]
    n_steps = n_pages // PAGES_PER_STEP
    ck = _positions_minor(cache_k)
    cv = _positions_minor(cache_v)
    qt = _query_tiles(q, dec_batch, HEAD_DIM, C_ROWS, _c_row)
    kn = _new_page(k, dec_batch)
    vn = _new_page(v, dec_batch)
    valid = np.zeros((C_QROWS, PAGE_SIZE), np.float32)
    for c in range(N_VGROUPS):
        for hl in range(HEADS_PER_GROUP):
            for t in range(DEC_SEQ):
                valid[c * C_ROWS + _c_row(hl, t), :t] = 1.0
    page = (1, WIDTH, PAGE_SIZE)

    def page_map(g):
        return lambda b, p, pt: (j * n_pool + pt[b, n_pages - 1 - (p * PAGES_PER_STEP + g)], 0, 0)

    per_b = lambda b, p, pt: (b, 0, 0)
    in_specs = [pl.BlockSpec((1, N_VGROUPS, C_ROWS, MXU_DIM), lambda b, p, pt: (b, 0, 0, 0))]
    in_specs += [pl.BlockSpec(page, page_map(g)) for g in range(PAGES_PER_STEP)] * 2
    in_specs += [pl.BlockSpec(page, per_b), pl.BlockSpec(page, per_b),
                 pl.BlockSpec((C_QROWS, PAGE_SIZE), lambda b, p, pt: (0, 0)),
                 pl.BlockSpec((PAGE_SIZE, PAGE_SIZE), lambda b, p, pt: (0, 0))]
    grid_spec = pltpu.PrefetchScalarGridSpec(
        num_scalar_prefetch=1,
        grid=(dec_batch, n_steps),
        in_specs=in_specs,
        out_specs=pl.BlockSpec((1, N_VGROUPS, C_ROWS, MXU_DIM), lambda b, p, pt: (b, 0, 0, 0)),
        scratch_shapes=[pltpu.VMEM((C_QROWS, 1), F32),
                        pltpu.VMEM((N_VGROUPS, C_ROWS, MXU_DIM), F32)])
    o = pl.pallas_call(
        functools.partial(_stick_sample_kernel, n_steps=n_steps),
        grid_spec=grid_spec,
        out_shape=jax.ShapeDtypeStruct((dec_batch, N_VGROUPS, C_ROWS, MXU_DIM), F32),
        compiler_params=_cparams(("parallel", "arbitrary")),
        name="stick_sample",
    )(page_table, qt, *([ck] * PAGES_PER_STEP), *([cv] * PAGES_PER_STEP), kn, vn, jnp.asarray(valid),
      _tri_later(PAGE_SIZE))
    return _diag_blocks(o, dec_batch)


PROMPT_TM = 1024


def _heads_last(xt, batch, seq, n_sub, sub_dim):
    return jnp.transpose(xt.reshape(batch, n_sub, sub_dim, seq), (0, 3, 1, 2))


def kernel(x_prompt, x_sample, cache_a_k_g0, cache_a_v_g0, cache_a_k_g1, cache_a_v_g1, cache_a_k_g2, cache_a_v_g2, cache_b_k, cache_b_v, cache_c_k, cache_c_v, page_table, norm_gain, rel_bias, w_in_a, w_out_a, q_gain_a, k_gain_a, w_in_b, w_out_b, q_gain_b, k_gain_b, lambda_q1, lambda_k1, lambda_q2, lambda_k2, sub_gain_b, w_in_c, w_out_c):
    batch, seq, _ = x_prompt.shape
    dec_batch = x_sample.shape[0]
    depth = norm_gain.shape[0]
    n_p = batch * seq
    n_s = dec_batch * DEC_SEQ
    caches_ak = (cache_a_k_g0, cache_a_k_g1, cache_a_k_g2)
    caches_av = (cache_a_v_g0, cache_a_v_g1, cache_a_v_g2)

    bias_tab = _bias_table(rel_bias)
    tabs_a = _dil_prompt_tables(bias_tab)
    tps_a, tn_a = _dil_sample_tables(bias_tab)
    ext_b = _diff_prompt_bias_rows(bias_tab, seq)
    bias_sb = _diff_sample_bias(bias_tab, page_table.shape[1] * PAGE_SIZE)

    y_p = x_prompt.reshape(n_p, D_MODEL)
    y_s = x_sample.reshape(n_s, D_MODEL)
    a_kp = [[] for _ in DILATED_CONFIGS]
    a_vp = [[] for _ in DILATED_CONFIGS]
    a_ks, a_vs = [], []
    b_kp, b_vp, b_ks, b_vs = [], [], [], []
    c_kp, c_vp, c_ks, c_vs = [], [], [], []
    no_t = ("n", "n", "n", "n")

    for i in range(depth):
        kind, j = i % 3, i // 3
        g = norm_gain[i][None, :]
        if kind == 0:
            w = w_in_a[j].astype(BF16)
            wq, wk, wv, wg = (w[:, :WIDTH], w[:, WIDTH:2 * WIDTH], w[:, 2 * WIDTH:3 * WIDTH], w[:, 3 * WIDTH:])
            gq = jnp.tile(q_gain_a[j], N_HEADS)[None, :]
            gk = jnp.tile(k_gain_a[j], N_HEADS)[None, :]
            w_out = w_out_a[j].astype(BF16)
            norms = (HEAD_DIM, HEAD_DIM, 0, 0)
            wa = [wq, wk, wv, wg]
            q, k, kt, v, vt, gate = _proj_in(y_p, g, wa, [gq, gk], norms, ("n", "nt", "nt", "n1"), PROMPT_TM, seq)
            os_, ls_ = [], []
            for gi in range(len(DILATED_CONFIGS)):
                o_g, l_g = _dil_prompt_group(q, k, v, tabs_a[gi], gi, batch, seq)
                os_.append(o_g)
                ls_.append(l_g)
            y_p = _proj_out(os_ + ls_, gate, y_p, w_out, PROMPT_TM)
            for gi, (window, _) in enumerate(DILATED_CONFIGS):
                keep = min(window, seq)
                fs = slice(gi * A_OUT, (gi + 1) * A_OUT)
                a_kp[gi].append(_heads_last(kt[:, fs, seq - keep:], batch, keep, H_GROUP, HEAD_DIM))
                a_vp[gi].append(_heads_last(vt[:, fs, seq - keep:], batch, keep, H_GROUP, HEAD_DIM))
            q, k, v, gate = _proj_in(y_s, g, wa, [gq, gk], norms, ("n", "n", "n", "n1"), n_s, n_s)
            o = _dil_sample(q, k, v, caches_ak, caches_av, j, tps_a, tn_a, dec_batch)
            y_s = _proj_out([o], gate, y_s, w_out, n_s)
            a_ks.append(k.reshape(dec_batch, DEC_SEQ, N_HEADS, HEAD_DIM))
            a_vs.append(v.reshape(dec_batch, DEC_SEQ, N_HEADS, HEAD_DIM))
        else:
            w_in = (w_in_b if kind == 1 else w_in_c)[j].astype(BF16)
            w_out = (w_out_b if kind == 1 else w_out_c)[j].astype(BF16)
            ws = [w_in[:, c * WIDTH:(c + 1) * WIDTH] for c in range(4)]
            ws_p = [ws[0].T, ws[1].T, ws[2].T, ws[3]]
            or_p = ("t", "t", "t", "n")
            if kind == 1:
                lambda_init = float(0.8 - 0.6 * math.exp(-0.3 * i))
                gq = jnp.tile(q_gain_b[j], 2 * N_HEADS)[None, :]
                gk = jnp.tile(k_gain_b[j], 2 * N_HEADS)[None, :]
                groups = (DIFF_DIM, DIFF_DIM, 0, 0)
                gains_p, gains_s = [gq.T, gk.T], [gq, gk]
                lam_vecs = jnp.stack([lambda_q1[j], lambda_k1[j], lambda_q2[j], lambda_k2[j]])
                sub_gain = sub_gain_b[j][None, :]
            else:
                groups = (0, 0, 0, 0)
                gains_p, gains_s = [], []
            q_scale = DIFF_DIM ** -0.5 * LOG2E if kind == 1 else HEAD_DIM ** -0.5
            qt, kt, vt, gate = _proj_in(y_p, g, ws_p, gains_p, groups, or_p, PROMPT_TM, seq,
                                        post=(q_scale, None, None, None))
            qs, ks_, vs_, gate_s = _proj_in(y_s, g, ws, gains_s, groups, no_t, n_s, n_s)
            if kind == 1:
                o = _diff_prompt(qt, kt, vt, ext_b, lam_vecs, sub_gain.T, lambda_init, batch, seq)
                o_s = _diff_sample(qs, ks_, vs_, cache_b_k, cache_b_v, j, page_table, bias_sb, lam_vecs,
                                   sub_gain, lambda_init, dec_batch)
                b_kp.append(_heads_last(kt, batch, seq, 2 * N_HEADS, DIFF_DIM))
                b_vp.append(_heads_last(vt, batch, seq, N_HEADS, HEAD_DIM))
                b_ks.append(ks_.reshape(dec_batch, DEC_SEQ, 2 * N_HEADS, DIFF_DIM))
                b_vs.append(vs_.reshape(dec_batch, DEC_SEQ, N_HEADS, HEAD_DIM))
            else:
                o = _stick_prompt(qt, kt, vt, batch, seq)
                o_s = _stick_sample(qs, ks_, vs_, cache_c_k, cache_c_v, j, page_table, dec_batch)
                c_kp.append(_heads_last(kt, batch, seq, N_HEADS, HEAD_DIM))
                c_vp.append(_heads_last(vt, batch, seq, N_HEADS, HEAD_DIM))
                c_ks.append(ks_.reshape(dec_batch, DEC_SEQ, N_HEADS, HEAD_DIM))
                c_vs.append(vs_.reshape(dec_batch, DEC_SEQ, N_HEADS, HEAD_DIM))
            y_p = _proj_out([o], gate, y_p, w_out, PROMPT_TM)
            y_s = _proj_out([o_s], gate_s, y_s, w_out, n_s)

    return (y_p.reshape(batch, seq, D_MODEL), y_s.reshape(dec_batch, DEC_SEQ, D_MODEL),
            jnp.stack(a_kp[0]), jnp.stack(a_vp[0]), jnp.stack(a_kp[1]), jnp.stack(a_vp[1]),
            jnp.stack(a_kp[2]), jnp.stack(a_vp[2]), jnp.stack(a_ks), jnp.stack(a_vs),
            jnp.stack(b_kp), jnp.stack(b_vp), jnp.stack(b_ks), jnp.stack(b_vs),
            jnp.stack(c_kp), jnp.stack(c_vp), jnp.stack(c_ks), jnp.stack(c_vs))
```
